```python
import math
import jax, jax.numpy as jnp
from jax import lax
import numpy as np


D_MODEL = 2048
BATCH = 1
SEQ = 8192
DEPTH = 4

N_MIXERS = 3
D_FF = 5632
FFN_RES = 0.5
NORM_EPS = 1e-6
NEG = -1e30
N_MOD = 9

MOBA_HEADS = 16
MOBA_HEAD_DIM = D_MODEL // MOBA_HEADS
MOBA_BLOCK = 256
MOBA_TOPK = 3
MOBA_QCHUNK = 64

POOL_WINDOWS = (2, 4, 8, 16)
POOL_GROUP = D_MODEL // len(POOL_WINDOWS)

SWA_HEAD_DIM = 64
SWA_Q_HEADS = D_MODEL // SWA_HEAD_DIM
SWA_KV_HEADS = SWA_Q_HEADS // 8
SWA_WINDOW = 128

N_A = (DEPTH + 2) // 3
N_B = (DEPTH + 1) // 3
N_C = DEPTH // 3

kernel_name = "hybrid_moba_pool_swa_macaron_adaln"


def rms_norm(x, g):
    xf = x.astype(jnp.float32)
    y = xf * lax.rsqrt(jnp.mean(xf * xf, axis=-1, keepdims=True) + NORM_EPS)
    return (y * g.astype(jnp.float32)).astype(x.dtype)


def modulate(h, shift, scale):
    return h * (1.0 + scale[:, None, :]) + shift[:, None, :]


def alibi_slopes(n):
    return jnp.asarray([2.0 ** (-8.0 * (i + 1) / n) for i in range(n)], jnp.float32)


def swiglu(h, w_gate, w_up, w_down):
    return (jax.nn.silu(h @ w_gate) * (h @ w_up)) @ w_down


def moba_attention(h, w_qkv, w_o):
    B, T, _ = h.shape
    H, dh, BS, QC = MOBA_HEADS, MOBA_HEAD_DIM, MOBA_BLOCK, MOBA_QCHUNK
    q, k, v = jnp.split(h @ w_qkv, 3, axis=-1)
    Tp = -(-T // BS) * BS
    pad = Tp - T

    def heads(a):
        a = jnp.pad(a, ((0, 0), (0, pad), (0, 0)))
        return a.reshape(B, Tp, H, dh).transpose(0, 2, 1, 3)

    q, k, v = heads(q), heads(k), heads(v)
    NB = Tp // BS
    kb = k.reshape(B, H, NB, BS, dh)
    vb = v.reshape(B, H, NB, BS, dh)
    kmean = jnp.mean(kb.astype(jnp.float32), axis=3)
    gate = jnp.einsum('bhtd,bhnd->bhtn', q.astype(jnp.float32), kmean)
    qblk = jnp.arange(Tp) // BS
    past = jnp.arange(NB)[None, :] < qblk[:, None]
    gate = jnp.where(past, gate, NEG)
    kk = min(MOBA_TOPK, NB)
    _, sel = lax.top_k(gate, kk)
    sel_valid = sel < qblk[None, None, :, None]
    slopes = alibi_slopes(H)
    scale = dh ** -0.5
    b_ix = jnp.arange(B)[:, None, None, None]
    h_ix = jnp.arange(H)[None, :, None, None]

    def chunk(ci):
        t0 = ci * QC
        qc = lax.dynamic_slice_in_dim(q, t0, QC, axis=2)
        sc = lax.dynamic_slice_in_dim(sel, t0, QC, axis=2)
        vc = lax.dynamic_slice_in_dim(sel_valid, t0, QC, axis=2)
        tpos = t0 + jnp.arange(QC)
        ksel = kb[b_ix, h_ix, sc]
        vsel = vb[b_ix, h_ix, sc]
        spos = sc[..., None] * BS + jnp.arange(BS)
        s_sel = jnp.einsum('bhqd,bhqnkd->bhqnk', qc, ksel,
                           preferred_element_type=jnp.float32) * scale
        s_sel = s_sel - slopes[None, :, None, None, None] * (
            tpos[None, None, :, None, None] - spos).astype(jnp.float32)
        s_sel = jnp.where(vc[..., None], s_sel, NEG)
        ob = t0 // BS
        kown = lax.dynamic_slice_in_dim(kb, ob, 1, axis=2)[:, :, 0]
        vown = lax.dynamic_slice_in_dim(vb, ob, 1, axis=2)[:, :, 0]
        dist = tpos[:, None] - (ob * BS + jnp.arange(BS))[None, :]
        s_own = jnp.einsum('bhqd,bhkd->bhqk', qc, kown,
                           preferred_element_type=jnp.float32) * scale
        s_own = s_own - slopes[None, :, None, None] * dist.astype(jnp.float32)
        s_own = jnp.where(dist >= 0, s_own, NEG)
        s_all = jnp.concatenate([s_sel.reshape(B, H, QC, kk * BS), s_own], axis=-1)
        p = jax.nn.softmax(s_all, axis=-1).astype(v.dtype)
        p_sel = p[..., :kk * BS].reshape(B, H, QC, kk, BS)
        p_own = p[..., kk * BS:]
        return (jnp.einsum('bhqnk,bhqnkd->bhqd', p_sel, vsel)
                + jnp.einsum('bhqk,bhkd->bhqd', p_own, vown))

    outs = lax.map(chunk, jnp.arange(Tp // QC))
    o = outs.transpose(1, 0, 3, 2, 4).reshape(B, Tp, H * dh)[:, :T]
    return o @ w_o


def pool_mixer(h, w_pool, pool_scale):
    B, T, D = h.shape
    G = POOL_GROUP
    hf = h.astype(jnp.float32)
    cs = jnp.concatenate([jnp.zeros((B, 1, D), jnp.float32), lax.cumsum(hf, axis=1)], axis=1)
    t = jnp.arange(T)
    outs = []
    for gi, w in enumerate(POOL_WINDOWS):
        csg = cs[..., gi * G:(gi + 1) * G]
        lo = jnp.concatenate([jnp.zeros((B, w - 1, G), jnp.float32), csg[:, :T - w + 1]], axis=1)
        cnt = jnp.minimum(t + 1, w).astype(jnp.float32)
        outs.append((csg[:, 1:] - lo) / cnt[None, :, None] - hf[..., gi * G:(gi + 1) * G])
    pooled = jnp.stack(outs, axis=2).astype(h.dtype)
    mixed = jnp.einsum('btgi,gio->btgo', pooled, w_pool).reshape(B, T, D)
    return mixed * pool_scale


def swa_sink_attention(h, w_qkv, w_o, sinks):
    B, T, _ = h.shape
    Hq, Hkv, dh, W = SWA_Q_HEADS, SWA_KV_HEADS, SWA_HEAD_DIM, SWA_WINDOW
    G = Hq // Hkv
    NBq = T // W
    qkv = h @ w_qkv
    q = qkv[..., :Hq * dh].reshape(B, NBq, W, Hkv, G, dh)
    k = qkv[..., Hq * dh:(Hq + Hkv) * dh].reshape(B, NBq, W, Hkv, dh)
    v = qkv[..., (Hq + Hkv) * dh:].reshape(B, NBq, W, Hkv, dh)

    def band(a):
        prev = jnp.pad(a, ((0, 0), (1, 0), (0, 0), (0, 0), (0, 0)))[:, :-1]
        return jnp.concatenate([prev, a], axis=2)

    kband, vband = band(k), band(v)
    s = jnp.einsum('bnqkgd,bnskd->bkgnqs', q, kband,
                   preferred_element_type=jnp.float32) * (dh ** -0.5)
    dist = (jnp.arange(W)[:, None] + W) - jnp.arange(2 * W)[None, :]
    kpos = jnp.arange(NBq)[:, None] * W - W + jnp.arange(2 * W)[None, :]
    valid = ((dist >= 0) & (dist < W))[None] & (kpos >= 0)[:, None, :]
    slopes = alibi_slopes(Hq).reshape(Hkv, G)
    s = s - slopes[None, :, :, None, None, None] * dist.astype(jnp.float32)
    s = jnp.where(valid, s, NEG)
    sink = jnp.broadcast_to(sinks.astype(jnp.float32).reshape(1, Hkv, G, 1, 1, 1), s.shape[:-1] + (1,))
    p = jax.nn.softmax(jnp.concatenate([s, sink], axis=-1), axis=-1)[..., :-1]
    o = jnp.einsum('bkgnqs,bnskd->bnqkgd', p.astype(h.dtype), vband).reshape(B, T, Hq * dh)
    return o @ w_o


def setup_inputs(seed: int = 0) -> dict:
    key = jax.random.key(seed)
    ks = jax.random.split(key, 16)
    D, F, G = D_MODEL, D_FF, POOL_GROUP
    swa_qkv_out = (SWA_Q_HEADS + 2 * SWA_KV_HEADS) * SWA_HEAD_DIM
    nrm = jax.random.normal
    return {
        'x': nrm(ks[0], (BATCH, SEQ, D), jnp.float32),
        'c': nrm(ks[1], (BATCH, D), jnp.float32),
        'norm_g': 1.0 + 0.02 * nrm(ks[2], (DEPTH, 3, D), jnp.float32),
        'ada_w': nrm(ks[3], (DEPTH, D, N_MOD * D), jnp.float32) * (0.5 * D ** -0.5),
        'ada_b': 0.01 * nrm(ks[4], (DEPTH, N_MOD * D), jnp.float32),
        'ffn_w_gate': nrm(ks[5], (DEPTH, 2, D, F), jnp.float32) * D ** -0.5,
        'ffn_w_up': nrm(ks[6], (DEPTH, 2, D, F), jnp.float32) * D ** -0.5,
        'ffn_w_down': nrm(ks[7], (DEPTH, 2, F, D), jnp.float32) * F ** -0.5,
        'moba_w_qkv': nrm(ks[8], (N_A, D, 3 * D), jnp.float32) * D ** -0.5,
        'moba_w_o': nrm(ks[9], (N_A, D, D), jnp.float32) * D ** -0.5,
        'pool_w': nrm(ks[10], (N_B, len(POOL_WINDOWS), G, G), jnp.float32) * G ** -0.5,
        'pool_scale': 1.0 + 0.02 * nrm(ks[11], (N_B, D), jnp.float32),
        'swa_w_qkv': nrm(ks[12], (N_C, D, swa_qkv_out), jnp.float32) * D ** -0.5,
        'swa_w_o': nrm(ks[13], (N_C, SWA_Q_HEADS * SWA_HEAD_DIM, D), jnp.float32) * (SWA_Q_HEADS * SWA_HEAD_DIM) ** -0.5,
        'swa_sinks': 0.5 * nrm(ks[14], (N_C, SWA_Q_HEADS), jnp.float32),
        'final_g': 1.0 + 0.02 * nrm(ks[15], (D,), jnp.float32),
    }


def reference(x, c, norm_g, ada_w, ada_b, ffn_w_gate, ffn_w_up, ffn_w_down,
              moba_w_qkv, moba_w_o, pool_w, pool_scale,
              swa_w_qkv, swa_w_o, swa_sinks, final_g):
    cs = jax.nn.silu(c)
    for i in range(DEPTH):
        mod = cs @ ada_w[i] + ada_b[i]
        sh1, sc1, g1, sh2, sc2, g2, sh3, sc3, g3 = jnp.split(mod, N_MOD, axis=-1)
        h = modulate(rms_norm(x, norm_g[i, 0]), sh1, sc1)
        x = x + FFN_RES * g1[:, None, :] * swiglu(h, ffn_w_gate[i, 0], ffn_w_up[i, 0], ffn_w_down[i, 0])
        h = modulate(rms_norm(x, norm_g[i, 1]), sh2, sc2)
        kind, j = i % N_MIXERS, i // N_MIXERS
        if kind == 0:
            y = moba_attention(h, moba_w_qkv[j], moba_w_o[j])
        elif kind == 1:
            y = pool_mixer(h, pool_w[j], pool_scale[j])
        else:
            y = swa_sink_attention(h, swa_w_qkv[j], swa_w_o[j], swa_sinks[j])
        x = x + g2[:, None, :] * y
        h = modulate(rms_norm(x, norm_g[i, 2]), sh3, sc3)
        x = x + FFN_RES * g3[:, None, :] * swiglu(h, ffn_w_gate[i, 1], ffn_w_up[i, 1], ffn_w_down[i, 1])
    return rms_norm(x, final_g)
```

```python
import functools

import numpy as np
import jax
import jax.numpy as jnp
from jax import lax
from jax.experimental import pallas as pl
from jax.experimental.pallas import tpu as pltpu

F32 = jnp.float32
BF16 = jnp.bfloat16

NORM_EPS = 1e-6
FFN_RES = 0.5
MASKED = -1e30

LANES = 128
V7X_VMEM_BYTES = 64 * 1024 * 1024

MOBA_HEADS = 16
MOBA_HEAD_DIM = 128
MOBA_BLOCK = 256
MOBA_TOPK = 3
MOBA_QTILE = 2 * MOBA_BLOCK

POOL_WINDOWS = (2, 4, 8, 16)
POOL_HALO = 16

SWA_HEAD_DIM = 64
SWA_Q_HEADS = 32
SWA_KV_HEADS = 4
SWA_GROUP = SWA_Q_HEADS // SWA_KV_HEADS
SWA_WINDOW = 128

NT_DIMS = (((1,), (1,)), ((), ()))


def _params(semantics, vmem_mib):
    return pltpu.CompilerParams(dimension_semantics=semantics,
                                vmem_limit_bytes=vmem_mib * 1024 * 1024)


def _norm_mod(xf, g, shift, scale):
    ms = jnp.mean(xf * xf, axis=-1, keepdims=True)
    y = xf * lax.rsqrt(ms + NORM_EPS)
    return (y * g) * (1.0 + scale) + shift


def _split_bf16(v):
    hi = v.astype(BF16).astype(F32)
    lo = (v - hi).astype(BF16).astype(F32)
    return hi, lo


def _ada_kernel(c_ref, w_ref, b_ref, o_ref):
    cb = c_ref[...]
    cs = cb * jax.nn.sigmoid(cb)
    w = w_ref[...]
    prod = w * pltpu.repeat(cs, w.shape[1] // LANES, axis=1)
    o_ref[...] = jnp.sum(prod, axis=0, keepdims=True) + b_ref[...]


def _ada_mod(c, ada_w, ada_b, tn=1024):
    depth, d, n = ada_w.shape
    c_rep = jnp.broadcast_to(c.reshape(d, 1), (d, LANES))
    out = pl.pallas_call(
        _ada_kernel,
        grid=(depth, n // tn),
        in_specs=[
            pl.BlockSpec((d, LANES), lambda l, j: (0, 0)),
            pl.BlockSpec((None, d, tn), lambda l, j: (l, 0, j)),
            pl.BlockSpec((None, 1, tn), lambda l, j: (l, 0, j)),
        ],
        out_specs=pl.BlockSpec((None, 1, tn), lambda l, j: (l, 0, j)),
        out_shape=jax.ShapeDtypeStruct((depth, 1, n), F32),
        compiler_params=_params(("arbitrary", "arbitrary"), 40),
        name="ada_mod",
    )(c_rep, ada_w, ada_b.reshape(depth, 1, n))
    return out.reshape(depth, n // d, d)


def _ffn_kernel(x_ref, g_ref, mod_ref, wg_ref, wu_ref, wd_ref, fg_ref, o_ref,
                h_ref, acc_ref, *, final_norm):
    j = pl.program_id(1)

    @pl.when(j == 0)
    def _():
        h = _norm_mod(x_ref[...], g_ref[...], mod_ref[0:1, :], mod_ref[1:2, :])
        h_ref[...] = h.astype(BF16)

    h = h_ref[...]
    gg = jnp.dot(h, wg_ref[...], preferred_element_type=F32)
    uu = jnp.dot(h, wu_ref[...], preferred_element_type=F32)
    a = ((gg * jax.nn.sigmoid(gg)) * uu).astype(BF16)
    part = jnp.dot(a, wd_ref[...], preferred_element_type=F32)

    @pl.when(j == 0)
    def _():
        acc_ref[...] = part

    @pl.when(j > 0)
    def _():
        acc_ref[...] += part

    @pl.when(j == pl.num_programs(1) - 1)
    def _():
        out = x_ref[...] + (FFN_RES * mod_ref[2:3, :]) * acc_ref[...]
        if final_norm:
            ms = jnp.mean(out * out, axis=-1, keepdims=True)
            out = (out * lax.rsqrt(ms + NORM_EPS)) * fg_ref[...]
        o_ref[...] = out


def _ffn(x, g, mod3, wg, wu, wd, final_g, *, final_norm, tm=512, tf=512):
    t, d = x.shape
    f = wg.shape[1]
    return pl.pallas_call(
        functools.partial(_ffn_kernel, final_norm=final_norm),
        grid=(t // tm, f // tf),
        in_specs=[
            pl.BlockSpec((tm, d), lambda i, j: (i, 0)),
            pl.BlockSpec((1, d), lambda i, j: (0, 0)),
            pl.BlockSpec((3, d), lambda i, j: (0, 0)),
            pl.BlockSpec((d, tf), lambda i, j: (0, j)),
            pl.BlockSpec((d, tf), lambda i, j: (0, j)),
            pl.BlockSpec((tf, d), lambda i, j: (j, 0)),
            pl.BlockSpec((1, d), lambda i, j: (0, 0)),
        ],
        out_specs=pl.BlockSpec((tm, d), lambda i, j: (i, 0)),
        out_shape=jax.ShapeDtypeStruct((t, d), F32),
        scratch_shapes=[pltpu.VMEM((tm, d), BF16), pltpu.VMEM((tm, d), F32)],
        compiler_params=_params(("arbitrary", "arbitrary"), 48),
        name="ffn",
    )(x, g.reshape(1, d), mod3, wg, wu, wd, final_g.reshape(1, d))


def _nmm_kernel(x_ref, g_ref, mod_ref, w_ref, cs_ref, o_ref, h_ref):
    @pl.when(pl.program_id(1) == 0)
    def _():
        h = _norm_mod(x_ref[...], g_ref[...], mod_ref[0:1, :], mod_ref[1:2, :])
        h_ref[...] = h.astype(BF16)

    r = jnp.dot(h_ref[...], w_ref[...], preferred_element_type=F32)
    o_ref[...] = (r * cs_ref[...]).astype(o_ref.dtype)


def _norm_mod_matmul(x, g, mod3, w, colscale, *, tm=1024, tn=1024):
    t, d = x.shape
    n = w.shape[1]
    return pl.pallas_call(
        _nmm_kernel,
        grid=(t // tm, n // tn),
        in_specs=[
            pl.BlockSpec((tm, d), lambda i, j: (i, 0)),
            pl.BlockSpec((1, d), lambda i, j: (0, 0)),
            pl.BlockSpec((3, d), lambda i, j: (0, 0)),
            pl.BlockSpec((d, tn), lambda i, j: (0, j)),
            pl.BlockSpec((1, tn), lambda i, j: (0, j)),
        ],
        out_specs=pl.BlockSpec((tm, tn), lambda i, j: (i, j)),
        out_shape=jax.ShapeDtypeStruct((t, n), BF16),
        scratch_shapes=[pltpu.VMEM((tm, d), BF16)],
        compiler_params=_params(("arbitrary", "arbitrary"), 48),
        name="norm_mod_matmul",
    )(x, g.reshape(1, d), mod3, w, colscale.reshape(1, n))


def _proj_res_kernel(a_ref, w_ref, x_ref, gate_ref, o_ref):
    r = jnp.dot(a_ref[...], w_ref[...], preferred_element_type=F32)
    o_ref[...] = x_ref[...] + gate_ref[...] * r


def _proj_residual(a, w, x, gate, *, tm=512):
    t, k = a.shape
    d = w.shape[1]
    return pl.pallas_call(
        _proj_res_kernel,
        grid=(t // tm,),
        in_specs=[
            pl.BlockSpec((tm, k), lambda i: (i, 0)),
            pl.BlockSpec((k, d), lambda i: (0, 0)),
            pl.BlockSpec((tm, d), lambda i: (i, 0)),
            pl.BlockSpec((1, d), lambda i: (0, 0)),
        ],
        out_specs=pl.BlockSpec((tm, d), lambda i: (i, 0)),
        out_shape=jax.ShapeDtypeStruct((t, d), F32),
        compiler_params=_params(("arbitrary",), 48),
        name="proj_residual",
    )(a, w, x, gate)


def _moba_key_const(t):
    pos = np.arange(t)
    lane = np.arange(LANES)[None, :]
    blk = (pos // MOBA_BLOCK)[:, None]
    inblk = (pos % MOBA_BLOCK)[:, None].astype(np.float32)
    kc = np.where(lane < 64, (blk == (lane % 32)).astype(np.float32), 0.0)
    kc = np.where((lane == 64) | (lane == 65), 1.0, kc)
    kc = np.where((lane == 66) | (lane == 67), inblk, kc)
    return jnp.asarray(kc, dtype=BF16)


def _alibi_slopes(n):
    return np.asarray([2.0 ** (-8.0 * (i + 1) / n) for i in range(n)], np.float32)


def _moba_kernel(slope_ref, q_ref, k_ref, v_ref, kc_ref, o_ref,
                 kaug, vaug, qaug, kmean_s, m_s, acc_s, *, nblk):
    dh = MOBA_HEAD_DIM
    tq = MOBA_QTILE
    pair = pl.program_id(1)

    @pl.when(pair == 0)
    def _():
        k = k_ref[...]
        kaug[:, 0:dh] = k
        kaug[:, dh:2 * dh] = kc_ref[...]
        vaug[:, 0:dh] = v_ref[...]
        ones_col = lax.broadcasted_iota(jnp.int32, (vaug.shape[0], dh), 1) == 0
        vaug[:, dh:2 * dh] = jnp.where(ones_col, 1.0, 0.0).astype(BF16)
        kf = k.astype(F32).reshape(nblk, MOBA_BLOCK, dh)
        kmean_s[...] = jnp.zeros(kmean_s.shape, F32)
        kmean_s[0:nblk, :] = jnp.sum(kf, axis=1) * (1.0 / MOBA_BLOCK)

    q = q_ref[...]
    km_hi, km_lo = _split_bf16(kmean_s[...])
    gate = (lax.dot_general(q, km_hi.astype(BF16), NT_DIMS, preferred_element_type=F32)
            + lax.dot_general(q, km_lo.astype(BF16), NT_DIMS, preferred_element_type=F32))

    lane = lax.broadcasted_iota(jnp.int32, (tq, LANES), 1)
    row = lax.broadcasted_iota(jnp.int32, (tq, LANES), 0)
    lane_f = lane.astype(F32)
    qblk = 2 * pair + (row >= MOBA_BLOCK).astype(jnp.int32)
    neg_inf = -jnp.inf

    g = jnp.where(lane < qblk, gate, neg_inf)
    blk = lane & 31
    blk_f = blk.astype(F32)
    active = blk == qblk
    for _ in range(MOBA_TOPK):
        mx = jnp.max(g, axis=1, keepdims=True)
        cand = jnp.where(g == mx, lane_f, float(LANES))
        idx = jnp.min(cand, axis=1, keepdims=True)
        idx = jnp.where(mx > neg_inf, idx, float(LANES))
        active = jnp.logical_or(active, blk_f == idx)
        g = jnp.where(lane_f == idx, neg_inf, g)

    slope = slope_ref[...]
    offset = -slope * (tq * pair - MOBA_BLOCK * blk).astype(F32)
    off_hi, off_lo = _split_bf16(offset)
    row_hi, row_lo = _split_bf16(-slope * row.astype(F32))
    sl_hi, sl_lo = _split_bf16(jnp.broadcast_to(slope, (tq, LANES)))
    extra = jnp.where(lane < 32, jnp.where(active, off_hi, MASKED),
            jnp.where(lane < 64, jnp.where(active, off_lo, 0.0),
            jnp.where(lane == 64, row_hi,
            jnp.where(lane == 65, row_lo,
            jnp.where(lane == 66, sl_hi,
            jnp.where(lane == 67, sl_lo, 0.0))))))
    qaug[:, 0:dh] = q
    qaug[:, dh:2 * dh] = extra.astype(BF16)

    t0 = pl.multiple_of(pair * tq, tq)
    s = lax.dot_general(qaug[...], kaug[pl.ds(t0, tq), :], NT_DIMS,
                        preferred_element_type=F32)
    r2 = lax.broadcasted_iota(jnp.int32, (tq, tq), 0)
    c2 = lax.broadcasted_iota(jnp.int32, (tq, tq), 1)
    s = jnp.where(c2 <= r2, s, MASKED)
    m0 = jnp.max(s, axis=1, keepdims=True)
    p = jnp.exp(s - m0).astype(BF16)
    m_s[...] = m0
    acc_s[...] = jnp.dot(p, vaug[pl.ds(t0, tq), :], preferred_element_type=F32)

    def past_tile(c, carry):
        tc = pl.multiple_of(c * tq, tq)
        s = lax.dot_general(qaug[...], kaug[pl.ds(tc, tq), :], NT_DIMS,
                            preferred_element_type=F32)
        m_prev = m_s[...]
        m_new = jnp.maximum(m_prev, jnp.max(s, axis=1, keepdims=True))
        alpha = jnp.exp(m_prev - m_new)
        p = jnp.exp(s - m_new).astype(BF16)
        pv = jnp.dot(p, vaug[pl.ds(tc, tq), :], preferred_element_type=F32)
        acc_s[...] = acc_s[...] * alpha + pv
        m_s[...] = m_new
        return carry

    lax.fori_loop(0, pair, past_tile, 0)

    acc = acc_s[...]
    o_ref[...] = (acc[:, 0:dh] / acc[:, dh:dh + 1]).astype(o_ref.dtype)


def _moba_attention(qkv, t):
    h, dh, tq = MOBA_HEADS, MOBA_HEAD_DIM, MOBA_QTILE
    nblk = t // MOBA_BLOCK
    slopes = jnp.asarray(np.broadcast_to(_alibi_slopes(h)[:, None, None], (h, 1, LANES)))
    return pl.pallas_call(
        functools.partial(_moba_kernel, nblk=nblk),
        grid=(h, t // tq),
        in_specs=[
            pl.BlockSpec((None, 1, LANES), lambda hh, p: (hh, 0, 0)),
            pl.BlockSpec((tq, dh), lambda hh, p: (p, hh)),
            pl.BlockSpec((t, dh), lambda hh, p: (0, h + hh)),
            pl.BlockSpec((t, dh), lambda hh, p: (0, 2 * h + hh)),
            pl.BlockSpec((t, LANES), lambda hh, p: (0, 0)),
        ],
        out_specs=pl.BlockSpec((tq, dh), lambda hh, p: (p, hh)),
        out_shape=jax.ShapeDtypeStruct((t, h * dh), BF16),
        scratch_shapes=[
            pltpu.VMEM((t, 2 * dh), BF16),
            pltpu.VMEM((t, 2 * dh), BF16),
            pltpu.VMEM((tq, 2 * dh), BF16),
            pltpu.VMEM((LANES, dh), F32),
            pltpu.VMEM((tq, 1), F32),
            pltpu.VMEM((tq, 2 * dh), F32),
        ],
        compiler_params=_params(("arbitrary", "arbitrary"), 48),
        name="moba_attention",
    )(slopes, qkv, qkv, qkv, _moba_key_const(t))


def _pool_kernel(x_ref, xh_ref, g_ref, mod_ref, w_ref, ps_ref, o_ref, hbuf):
    i = pl.program_id(0)
    tm = x_ref.shape[0]
    grp = w_ref.shape[1]
    g, shift, scale, gate = g_ref[...], mod_ref[0:1, :], mod_ref[1:2, :], mod_ref[2:3, :]
    x = x_ref[...]
    h = _norm_mod(x, g, shift, scale)
    halo = _norm_mod(xh_ref[...], g, shift, scale)
    hbuf[0:POOL_HALO, :] = jnp.where(i == 0, 0.0, halo)
    hbuf[POOL_HALO:POOL_HALO + tm, :] = h
    tpos = i * tm + lax.broadcasted_iota(jnp.int32, (tm, 1), 0)
    for gi, win in enumerate(POOL_WINDOWS):
        cols = slice(gi * grp, (gi + 1) * grp)
        hg = h[:, cols]
        acc = hg
        for back in range(1, win):
            acc = acc + hbuf[POOL_HALO - back:POOL_HALO - back + tm, cols]
        cnt = jnp.minimum(tpos + 1, win).astype(F32)
        pooled = acc / cnt - hg
        mixed = jnp.dot(pooled.astype(BF16), w_ref[gi], preferred_element_type=F32)
        o_ref[:, cols] = x[:, cols] + gate[:, cols] * (mixed * ps_ref[:, cols])


def _pool_mixer(x, g, mod3, w_pool, pool_scale, *, tm=512):
    t, d = x.shape
    ng, grp, _ = w_pool.shape
    halo_blocks = tm // POOL_HALO
    return pl.pallas_call(
        _pool_kernel,
        grid=(t // tm,),
        in_specs=[
            pl.BlockSpec((tm, d), lambda i: (i, 0)),
            pl.BlockSpec((POOL_HALO, d), lambda i: (jnp.maximum(i * halo_blocks - 1, 0), 0)),
            pl.BlockSpec((1, d), lambda i: (0, 0)),
            pl.BlockSpec((3, d), lambda i: (0, 0)),
            pl.BlockSpec((ng, grp, grp), lambda i: (0, 0, 0)),
            pl.BlockSpec((1, d), lambda i: (0, 0)),
        ],
        out_specs=pl.BlockSpec((tm, d), lambda i: (i, 0)),
        out_shape=jax.ShapeDtypeStruct((t, d), F32),
        scratch_shapes=[pltpu.VMEM((POOL_HALO + tm, d), F32)],
        compiler_params=_params(("arbitrary",), 48),
        name="pool_mixer",
    )(x, x, g.reshape(1, d), mod3, w_pool, pool_scale.reshape(1, d))


def _swa_bias_const():
    w = SWA_WINDOW
    r = np.arange(w)[:, None]
    c = np.arange(2 * w)[None, :]
    dist = (r + w - c).astype(np.float32)
    valid = (dist >= 0) & (dist < w)
    slopes = _alibi_slopes(SWA_Q_HEADS)
    npair = SWA_GROUP // 2
    out = np.empty((SWA_KV_HEADS, npair * w, 2 * 2 * w), np.float32)
    for kv in range(SWA_KV_HEADS):
        for p in range(npair):
            for half in range(2):
                sl = slopes[kv * SWA_GROUP + 2 * p + half]
                tile = np.where(valid, -sl * dist, np.float32(MASKED)).astype(np.float32)
                out[kv, p * w:(p + 1) * w, half * 2 * w:(half + 1) * 2 * w] = tile
    return jnp.asarray(out)


def _swa_kernel(q_ref, kp_ref, kc_ref, vp_ref, vc_ref, bias_ref, sink_ref, o_ref):
    n = pl.program_id(0)
    w = SWA_WINDOW
    npair = SWA_GROUP // 2
    lane = lax.broadcasted_iota(jnp.int32, (2 * w, LANES), 1)
    low = lane < SWA_HEAD_DIM
    col = lax.broadcasted_iota(jnp.int32, (npair * w, 4 * w), 1)
    prev_cols = (col & (2 * w - 1)) < w
    no_prev = jnp.logical_and(n == 0, prev_cols)
    out_lane = lax.broadcasted_iota(jnp.int32, (npair * w, LANES), 1)
    for kv in range(SWA_KV_HEADS):
        ksl = slice(kv * LANES, (kv + 1) * LANES)
        kd = jnp.concatenate([kp_ref[:, ksl], kc_ref[:, ksl]], axis=0).astype(F32)
        vd = jnp.concatenate([vp_ref[:, ksl], vc_ref[:, ksl]], axis=0).astype(F32)
        kk = jnp.concatenate([jnp.where(low, kd, 0.0), jnp.where(low, 0.0, kd)], axis=0).astype(BF16)
        vv = jnp.concatenate([jnp.where(low, vd, 0.0), jnp.where(low, 0.0, vd)], axis=0).astype(BF16)
        base = kv * npair * LANES
        qs = jnp.concatenate(
            [q_ref[:, base + p * LANES:base + (p + 1) * LANES] for p in range(npair)], axis=0)
        s = lax.dot_general(qs, kk, NT_DIMS, preferred_element_type=F32) + bias_ref[kv]
        s = jnp.where(no_prev, MASKED, s)
        es, ls = [], []
        for half in range(2):
            sh = s[:, half * 2 * w:(half + 1) * 2 * w]
            sink = sink_ref[kv, :, half:half + 1]
            m = jnp.maximum(jnp.max(sh, axis=1, keepdims=True), sink)
            e = jnp.exp(sh - m)
            es.append(e)
            ls.append(jnp.sum(e, axis=1, keepdims=True) + jnp.exp(sink - m))
        p = jnp.concatenate(es, axis=1).astype(BF16)
        o = jnp.dot(p, vv, preferred_element_type=F32)
        o = o / jnp.where(out_lane < SWA_HEAD_DIM, ls[0], ls[1])
        for pp in range(npair):
            o_ref[:, base + pp * LANES:base + (pp + 1) * LANES] = (
                o[pp * w:(pp + 1) * w, :].astype(o_ref.dtype))


def _swa_attention(q, kvdup, sinks, t):
    w = SWA_WINDOW
    dq = SWA_Q_HEADS * SWA_HEAD_DIM
    dkv = SWA_KV_HEADS * LANES
    npair = SWA_GROUP // 2
    sink_tab = jnp.broadcast_to(
        sinks.astype(F32).reshape(SWA_KV_HEADS, npair, 1, 2),
        (SWA_KV_HEADS, npair, w, 2)).reshape(SWA_KV_HEADS, npair * w, 2)
    prev = lambda n: jnp.maximum(n - 1, 0)
    return pl.pallas_call(
        _swa_kernel,
        grid=(t // w,),
        in_specs=[
            pl.BlockSpec((w, dq), lambda n: (n, 0)),
            pl.BlockSpec((w, dkv), lambda n: (prev(n), 0)),
            pl.BlockSpec((w, dkv), lambda n: (n, 0)),
            pl.BlockSpec((w, dkv), lambda n: (prev(n), 1)),
            pl.BlockSpec((w, dkv), lambda n: (n, 1)),
            pl.BlockSpec((SWA_KV_HEADS, npair * w, 4 * w), lambda n: (0, 0, 0)),
            pl.BlockSpec((SWA_KV_HEADS, npair * w, 2), lambda n: (0, 0, 0)),
        ],
        out_specs=pl.BlockSpec((w, dq), lambda n: (n, 0)),
        out_shape=jax.ShapeDtypeStruct((t, dq), BF16),
        compiler_params=_params(("arbitrary",), 48),
        name="swa_attention",
    )(q, kvdup, kvdup, kvdup, kvdup, _swa_bias_const(), sink_tab)


def kernel(x, c, norm_g, ada_w, ada_b, ffn_w_gate, ffn_w_up, ffn_w_down, moba_w_qkv, moba_w_o,
           pool_w, pool_scale, swa_w_qkv, swa_w_o, swa_sinks, final_g):
    b, t, d = x.shape
    assert b == 1
    depth = norm_g.shape[0]
    mods = _ada_mod(c, ada_w, ada_b)

    def ffn(xs, i, s, mod3, final_norm):
        return _ffn(xs, norm_g[i, 2 * s], mod3, ffn_w_gate[i, s].astype(BF16), ffn_w_up[i, s].astype(BF16),
                    ffn_w_down[i, s].astype(BF16), final_g, final_norm=final_norm)

    xs = x.reshape(t, d)
    for i in range(depth):
        mod = mods[i]
        last = i == depth - 1
        xs = ffn(xs, i, 0, mod[0:3], False)
        kind, j = i % 3, i // 3
        m2 = mod[3:6]
        gate2 = mod[5:6]
        if kind == 0:
            qscale = jnp.concatenate([jnp.full((d,), MOBA_HEAD_DIM ** -0.5, F32), jnp.ones((2 * d,), F32)])
            qkv = _norm_mod_matmul(xs, norm_g[i, 1], m2, moba_w_qkv[j].astype(BF16), qscale)
            o = _moba_attention(qkv, t)
            xs = _proj_residual(o, moba_w_o[j].astype(BF16), xs, gate2)
        elif kind == 1:
            xs = _pool_mixer(xs, norm_g[i, 1], m2, pool_w[j].astype(BF16), pool_scale[j])
        else:
            dq = SWA_Q_HEADS * SWA_HEAD_DIM
            hd = SWA_HEAD_DIM
            wqkv = swa_w_qkv[j]
            wq = wqkv[:, :dq]
            wk = wqkv[:, dq:dq + SWA_KV_HEADS * hd].reshape(d, SWA_KV_HEADS, 1, hd)
            wv = wqkv[:, dq + SWA_KV_HEADS * hd:].reshape(d, SWA_KV_HEADS, 1, hd)
            dup = lambda a: jnp.broadcast_to(a, (d, SWA_KV_HEADS, 2, hd)).reshape(d, SWA_KV_HEADS * 2 * hd)
            wkv = jnp.concatenate([dup(wk), dup(wv)], axis=1).astype(BF16)
            q = _norm_mod_matmul(xs, norm_g[i, 1], m2, wq.astype(BF16),
                                 jnp.full((dq,), hd ** -0.5, F32))
            kvdup = _norm_mod_matmul(xs, norm_g[i, 1], m2, wkv, jnp.ones((wkv.shape[1],), F32))
            o = _swa_attention(q, kvdup, swa_sinks[j], t)
            xs = _proj_residual(o, swa_w_o[j].astype(BF16), xs, gate2)
        xs = ffn(xs, i, 1, mod[6:9], last)
    return xs.reshape(b, t, d)
```

```python
import functools

import numpy as np
import jax
import jax.numpy as jnp
from jax import lax
from jax.experimental import pallas as pl
from jax.experimental.pallas import tpu as pltpu

F32 = jnp.float32
BF16 = jnp.bfloat16

NORM_EPS = 1e-6
FFN_RES = 0.5
MASKED = -1e30
LOG2E = 1.4426950408889634

LANES = 128
V7X_VMEM_BYTES = 64 * 1024 * 1024

MOBA_HEADS = 16
MOBA_HEAD_DIM = 128
MOBA_BLOCK = 256
MOBA_TOPK = 3
MOBA_QTILE = 2 * MOBA_BLOCK

POOL_WINDOWS = (2, 4, 8, 16)
POOL_HALO = 16

SWA_HEAD_DIM = 64
SWA_Q_HEADS = 32
SWA_KV_HEADS = 4
SWA_GROUP = SWA_Q_HEADS // SWA_KV_HEADS
SWA_WINDOW = 128

NT_DIMS = (((1,), (1,)), ((), ()))


def _params(semantics, vmem_mib):
    return pltpu.CompilerParams(dimension_semantics=semantics,
                                vmem_limit_bytes=vmem_mib * 1024 * 1024)


def _norm_mod(xf, g, shift, scale):
    ms = jnp.mean(xf * xf, axis=-1, keepdims=True)
    y = xf * lax.rsqrt(ms + NORM_EPS)
    return (y * g) * (1.0 + scale) + shift


def _split_bf16(v):
    hi = v.astype(BF16).astype(F32)
    lo = (v - hi).astype(BF16).astype(F32)
    return hi, lo


def _ada_kernel(c_ref, w_ref, b_ref, o_ref):
    cb = c_ref[...]
    cs = cb * jax.nn.sigmoid(cb)
    w = w_ref[...]
    prod = w * pltpu.repeat(cs, w.shape[1] // LANES, axis=1)
    o_ref[...] = jnp.sum(prod, axis=0, keepdims=True) + b_ref[...]


def _ada_mod(c, ada_w, ada_b, tn=1024):
    depth, d, n = ada_w.shape
    c_rep = jnp.broadcast_to(c.reshape(d, 1), (d, LANES))
    out = pl.pallas_call(
        _ada_kernel,
        grid=(depth, n // tn),
        in_specs=[
            pl.BlockSpec((d, LANES), lambda l, j: (0, 0)),
            pl.BlockSpec((None, d, tn), lambda l, j: (l, 0, j)),
            pl.BlockSpec((None, 1, tn), lambda l, j: (l, 0, j)),
        ],
        out_specs=pl.BlockSpec((None, 1, tn), lambda l, j: (l, 0, j)),
        out_shape=jax.ShapeDtypeStruct((depth, 1, n), F32),
        compiler_params=_params(("arbitrary", "arbitrary"), 40),
        name="ada_mod",
    )(c_rep, ada_w, ada_b.reshape(depth, 1, n))
    return out.reshape(depth, n // d, d)


def _ffn_kernel(x_ref, g_ref, mod_ref, wg_ref, wu_ref, wd_ref, fg_ref, o_ref,
                h_ref, acc_ref, *, final_norm):
    j = pl.program_id(1)

    @pl.when(j == 0)
    def _():
        h = _norm_mod(x_ref[...], g_ref[...], mod_ref[0:1, :], mod_ref[1:2, :])
        h_ref[...] = h.astype(BF16)
        acc_ref[...] = jnp.zeros(acc_ref.shape, F32)

    h = h_ref[...]
    gg = jnp.dot(h, wg_ref[...], preferred_element_type=F32)
    uu = jnp.dot(h, wu_ref[...], preferred_element_type=F32)
    a = ((gg * jax.nn.sigmoid(gg)) * uu).astype(BF16)
    acc_ref[...] += jnp.dot(a, wd_ref[...], preferred_element_type=F32)

    @pl.when(j == pl.num_programs(1) - 1)
    def _():
        out = x_ref[...] + (FFN_RES * mod_ref[2:3, :]) * acc_ref[...]
        if final_norm:
            ms = jnp.mean(out * out, axis=-1, keepdims=True)
            out = (out * lax.rsqrt(ms + NORM_EPS)) * fg_ref[...]
        o_ref[...] = out


def _ffn(x, g, mod3, wg, wu, wd, layer, slot, final_g, *, final_norm, tm=512, tf=512):
    t, d = x.shape
    f = wg.shape[3]
    return pl.pallas_call(
        functools.partial(_ffn_kernel, final_norm=final_norm),
        grid=(t // tm, f // tf),
        in_specs=[
            pl.BlockSpec((tm, d), lambda i, j: (i, 0)),
            pl.BlockSpec((1, d), lambda i, j: (0, 0)),
            pl.BlockSpec((3, d), lambda i, j: (0, 0)),
            pl.BlockSpec((None, None, d, tf), lambda i, j: (layer, slot, 0, j)),
            pl.BlockSpec((None, None, d, tf), lambda i, j: (layer, slot, 0, j)),
            pl.BlockSpec((None, None, tf, d), lambda i, j: (layer, slot, j, 0)),
            pl.BlockSpec((1, d), lambda i, j: (0, 0)),
        ],
        out_specs=pl.BlockSpec((tm, d), lambda i, j: (i, 0)),
        out_shape=jax.ShapeDtypeStruct((t, d), F32),
        scratch_shapes=[pltpu.VMEM((tm, d), BF16), pltpu.VMEM((tm, d), F32)],
        compiler_params=_params(("arbitrary", "arbitrary"), 48),
        name="ffn",
    )(x, g.reshape(1, d), mod3, wg, wu, wd, final_g.reshape(1, d))


def _nmm_kernel(x_ref, g_ref, mod_ref, w_ref, cs_ref, o_ref, h_ref):
    @pl.when(pl.program_id(1) == 0)
    def _():
        h = _norm_mod(x_ref[...], g_ref[...], mod_ref[0:1, :], mod_ref[1:2, :])
        h_ref[...] = h.astype(BF16)

    r = jnp.dot(h_ref[...], w_ref[...], preferred_element_type=F32)
    o_ref[...] = (r * cs_ref[...]).astype(o_ref.dtype)


def _norm_mod_matmul(x, g, mod3, w, widx, colscale, *, tm=1024, tn=1024):
    t, d = x.shape
    n = w.shape[2]
    return pl.pallas_call(
        _nmm_kernel,
        grid=(t // tm, n // tn),
        in_specs=[
            pl.BlockSpec((tm, d), lambda i, j: (i, 0)),
            pl.BlockSpec((1, d), lambda i, j: (0, 0)),
            pl.BlockSpec((3, d), lambda i, j: (0, 0)),
            pl.BlockSpec((None, d, tn), lambda i, j: (widx, 0, j)),
            pl.BlockSpec((1, tn), lambda i, j: (0, j)),
        ],
        out_specs=pl.BlockSpec((tm, tn), lambda i, j: (i, j)),
        out_shape=jax.ShapeDtypeStruct((t, n), BF16),
        scratch_shapes=[pltpu.VMEM((tm, d), BF16)],
        compiler_params=_params(("arbitrary", "arbitrary"), 48),
        name="norm_mod_matmul",
    )(x, g.reshape(1, d), mod3, w, colscale.reshape(1, n))


def _proj_res_kernel(a_ref, w_ref, x_ref, gate_ref, o_ref):
    r = jnp.dot(a_ref[...], w_ref[...], preferred_element_type=F32)
    o_ref[...] = x_ref[...] + gate_ref[...] * r


def _proj_residual(a, w, widx, x, gate, *, tm=512):
    t, k = a.shape
    d = w.shape[2]
    return pl.pallas_call(
        _proj_res_kernel,
        grid=(t // tm,),
        in_specs=[
            pl.BlockSpec((tm, k), lambda i: (i, 0)),
            pl.BlockSpec((None, k, d), lambda i: (widx, 0, 0)),
            pl.BlockSpec((tm, d), lambda i: (i, 0)),
            pl.BlockSpec((1, d), lambda i: (0, 0)),
        ],
        out_specs=pl.BlockSpec((tm, d), lambda i: (i, 0)),
        out_shape=jax.ShapeDtypeStruct((t, d), F32),
        compiler_params=_params(("arbitrary",), 48),
        name="proj_residual",
    )(a, w, x, gate)


def _moba_key_const(t):
    pos = np.arange(t)
    lane = np.arange(LANES)[None, :]
    blk = (pos // MOBA_BLOCK)[:, None]
    inblk = (pos % MOBA_BLOCK)[:, None].astype(np.float32)
    kc = np.where(lane < 64, (blk == (lane % 32)).astype(np.float32), 0.0)
    kc = np.where((lane == 64) | (lane == 65), 1.0, kc)
    kc = np.where((lane == 66) | (lane == 67), inblk, kc)
    return jnp.asarray(kc, dtype=BF16)


def _alibi_slopes(n):
    return np.asarray([2.0 ** (-8.0 * (i + 1) / n) for i in range(n)], np.float32)


def _moba_kernel(slope_ref, q_ref, k_ref, v_ref, kc_ref, o_ref,
                 kaug, vaug, qaug, kmean_s, m_s, acc_s, *, nblk):
    dh = MOBA_HEAD_DIM
    tq = MOBA_QTILE
    pair = pl.program_id(1)

    @pl.when(pair == 0)
    def _():
        k = k_ref[...]
        kaug[:, 0:dh] = k
        kaug[:, dh:2 * dh] = kc_ref[...]
        vaug[:, 0:dh] = v_ref[...]
        ones_col = lax.broadcasted_iota(jnp.int32, (vaug.shape[0], dh), 1) == 0
        vaug[:, dh:2 * dh] = jnp.where(ones_col, 1.0, 0.0).astype(BF16)
        kf = k.astype(F32).reshape(nblk, MOBA_BLOCK, dh)
        kmean_s[...] = jnp.zeros(kmean_s.shape, F32)
        kmean_s[0:nblk, :] = jnp.sum(kf, axis=1) * (1.0 / MOBA_BLOCK)

    q = q_ref[...]
    km_hi, km_lo = _split_bf16(kmean_s[...])
    gate = (lax.dot_general(q, km_hi.astype(BF16), NT_DIMS, preferred_element_type=F32)
            + lax.dot_general(q, km_lo.astype(BF16), NT_DIMS, preferred_element_type=F32))

    lane = lax.broadcasted_iota(jnp.int32, (tq, LANES), 1)
    row = lax.broadcasted_iota(jnp.int32, (tq, LANES), 0)
    lane_f = lane.astype(F32)
    qblk = 2 * pair + (row >= MOBA_BLOCK).astype(jnp.int32)
    neg_inf = -jnp.inf

    g = jnp.where(lane < qblk, gate, neg_inf)
    blk = lane & 31
    blk_f = blk.astype(F32)
    active = blk == qblk
    for _ in range(MOBA_TOPK):
        mx = jnp.max(g, axis=1, keepdims=True)
        cand = jnp.where(g == mx, lane_f, float(LANES))
        idx = jnp.min(cand, axis=1, keepdims=True)
        idx = jnp.where(mx > neg_inf, idx, float(LANES))
        active = jnp.logical_or(active, blk_f == idx)
        g = jnp.where(lane_f == idx, neg_inf, g)

    slope = slope_ref[...]
    offset = -slope * (tq * pair - MOBA_BLOCK * blk).astype(F32)
    off_hi, off_lo = _split_bf16(offset)
    row_hi, row_lo = _split_bf16(-slope * row.astype(F32))
    sl_hi, sl_lo = _split_bf16(jnp.broadcast_to(slope, (tq, LANES)))
    extra = jnp.where(lane < 32, jnp.where(active, off_hi, MASKED),
            jnp.where(lane < 64, jnp.where(active, off_lo, 0.0),
            jnp.where(lane == 64, row_hi,
            jnp.where(lane == 65, row_lo,
            jnp.where(lane == 66, sl_hi,
            jnp.where(lane == 67, sl_lo, 0.0))))))
    qaug[:, 0:dh] = q
    qaug[:, dh:2 * dh] = extra.astype(BF16)

    def scores(tile):
        tc = pl.multiple_of(tile * tq, tq)
        return lax.dot_general(qaug[...], kaug[pl.ds(tc, tq), :], NT_DIMS,
                               preferred_element_type=F32)

    t0 = pl.multiple_of(pair * tq, tq)
    s = scores(pair)
    r2 = lax.broadcasted_iota(jnp.int32, (tq, tq), 0)
    c2 = lax.broadcasted_iota(jnp.int32, (tq, tq), 1)
    s = jnp.where(c2 <= r2, s, MASKED)
    m0 = jnp.max(s, axis=1, keepdims=True)
    p = jnp.exp2(s - m0).astype(BF16)
    m_s[...] = m0
    acc_s[...] = jnp.dot(p, vaug[pl.ds(t0, tq), :], preferred_element_type=F32)

    def past_tile(c, s):
        s_next = scores(jnp.minimum(c + 1, pair))
        tc = pl.multiple_of(c * tq, tq)
        m_prev = m_s[...]
        m_new = jnp.maximum(m_prev, jnp.max(s, axis=1, keepdims=True))
        alpha = jnp.exp2(m_prev - m_new)
        p = jnp.exp2(s - m_new).astype(BF16)
        pv = jnp.dot(p, vaug[pl.ds(tc, tq), :], preferred_element_type=F32)
        acc_s[...] = acc_s[...] * alpha + pv
        m_s[...] = m_new
        return s_next

    lax.fori_loop(0, pair, past_tile, scores(0))

    acc = acc_s[...]
    o_ref[...] = (acc[:, 0:dh] / acc[:, dh:dh + 1]).astype(o_ref.dtype)


def _moba_attention(qkv, t):
    h, dh, tq = MOBA_HEADS, MOBA_HEAD_DIM, MOBA_QTILE
    nblk = t // MOBA_BLOCK
    slopes = _alibi_slopes(h).astype(np.float64) * LOG2E
    slopes = jnp.asarray(np.broadcast_to(slopes.astype(np.float32)[:, None, None], (h, 1, LANES)))
    return pl.pallas_call(
        functools.partial(_moba_kernel, nblk=nblk),
        grid=(h, t // tq),
        in_specs=[
            pl.BlockSpec((None, 1, LANES), lambda hh, p: (hh, 0, 0)),
            pl.BlockSpec((tq, dh), lambda hh, p: (p, hh)),
            pl.BlockSpec((t, dh), lambda hh, p: (0, h + hh)),
            pl.BlockSpec((t, dh), lambda hh, p: (0, 2 * h + hh)),
            pl.BlockSpec((t, LANES), lambda hh, p: (0, 0)),
        ],
        out_specs=pl.BlockSpec((tq, dh), lambda hh, p: (p, hh)),
        out_shape=jax.ShapeDtypeStruct((t, h * dh), BF16),
        scratch_shapes=[
            pltpu.VMEM((t, 2 * dh), BF16),
            pltpu.VMEM((t, 2 * dh), BF16),
            pltpu.VMEM((tq, 2 * dh), BF16),
            pltpu.VMEM((LANES, dh), F32),
            pltpu.VMEM((tq, 1), F32),
            pltpu.VMEM((tq, 2 * dh), F32),
        ],
        compiler_params=_params(("arbitrary", "arbitrary"), 48),
        name="moba_attention",
    )(slopes, qkv, qkv, qkv, _moba_key_const(t))


def _pool_kernel(x_ref, xh_ref, g_ref, mod_ref, w_ref, ps_ref, o_ref, hbuf):
    i = pl.program_id(0)
    tm = x_ref.shape[0]
    grp = w_ref.shape[1]
    g, shift, scale, gate = g_ref[...], mod_ref[0:1, :], mod_ref[1:2, :], mod_ref[2:3, :]
    x = x_ref[...]
    h = _norm_mod(x, g, shift, scale)
    halo = _norm_mod(xh_ref[...], g, shift, scale)
    hbuf[0:POOL_HALO, :] = jnp.where(i == 0, 0.0, halo)
    hbuf[POOL_HALO:POOL_HALO + tm, :] = h
    tpos = i * tm + lax.broadcasted_iota(jnp.int32, (tm, 1), 0)
    for gi, win in enumerate(POOL_WINDOWS):
        cols = slice(gi * grp, (gi + 1) * grp)
        hg = h[:, cols]
        acc = hg
        for back in range(1, win):
            acc = acc + hbuf[POOL_HALO - back:POOL_HALO - back + tm, cols]
        cnt = jnp.minimum(tpos + 1, win).astype(F32)
        pooled = acc / cnt - hg
        mixed = jnp.dot(pooled.astype(BF16), w_ref[gi], preferred_element_type=F32)
        o_ref[:, cols] = x[:, cols] + gate[:, cols] * (mixed * ps_ref[:, cols])


def _pool_mixer(x, g, mod3, w_pool, pool_scale, *, tm=512):
    t, d = x.shape
    ng, grp, _ = w_pool.shape
    halo_blocks = tm // POOL_HALO
    return pl.pallas_call(
        _pool_kernel,
        grid=(t // tm,),
        in_specs=[
            pl.BlockSpec((tm, d), lambda i: (i, 0)),
            pl.BlockSpec((POOL_HALO, d), lambda i: (jnp.maximum(i * halo_blocks - 1, 0), 0)),
            pl.BlockSpec((1, d), lambda i: (0, 0)),
            pl.BlockSpec((3, d), lambda i: (0, 0)),
            pl.BlockSpec((ng, grp, grp), lambda i: (0, 0, 0)),
            pl.BlockSpec((1, d), lambda i: (0, 0)),
        ],
        out_specs=pl.BlockSpec((tm, d), lambda i: (i, 0)),
        out_shape=jax.ShapeDtypeStruct((t, d), F32),
        scratch_shapes=[pltpu.VMEM((POOL_HALO + tm, d), F32)],
        compiler_params=_params(("arbitrary",), 48),
        name="pool_mixer",
    )(x, x, g.reshape(1, d), mod3, w_pool, pool_scale.reshape(1, d))


def _swa_bias_const():
    w = SWA_WINDOW
    r = np.arange(w)[:, None]
    c = np.arange(2 * w)[None, :]
    dist = (r + w - c).astype(np.float32)
    valid = (dist >= 0) & (dist < w)
    slopes = _alibi_slopes(SWA_Q_HEADS)
    npair = SWA_GROUP // 2
    out = np.empty((SWA_KV_HEADS, npair * w, 2 * 2 * w), np.float32)
    for kv in range(SWA_KV_HEADS):
        for p in range(npair):
            for half in range(2):
                sl = slopes[kv * SWA_GROUP + 2 * p + half]
                tile = np.where(valid, -sl * dist, np.float32(MASKED)).astype(np.float32)
                out[kv, p * w:(p + 1) * w, half * 2 * w:(half + 1) * 2 * w] = tile
    return jnp.asarray(out)


def _swa_kernel(q_ref, kp_ref, kc_ref, vp_ref, vc_ref, bias_ref, sink_ref, o_ref):
    n = pl.program_id(0)
    w = SWA_WINDOW
    npair = SWA_GROUP // 2
    lane = lax.broadcasted_iota(jnp.int32, (2 * w, LANES), 1)
    low = lane < SWA_HEAD_DIM
    col = lax.broadcasted_iota(jnp.int32, (npair * w, 4 * w), 1)
    prev_cols = (col & (2 * w - 1)) < w
    no_prev = jnp.logical_and(n == 0, prev_cols)
    out_lane = lax.broadcasted_iota(jnp.int32, (npair * w, LANES), 1)
    for kv in range(SWA_KV_HEADS):
        ksl = slice(kv * LANES, (kv + 1) * LANES)
        kd = jnp.concatenate([kp_ref[:, ksl], kc_ref[:, ksl]], axis=0).astype(F32)
        vd = jnp.concatenate([vp_ref[:, ksl], vc_ref[:, ksl]], axis=0).astype(F32)
        kk = jnp.concatenate([jnp.where(low, kd, 0.0), jnp.where(low, 0.0, kd)], axis=0).astype(BF16)
        vv = jnp.concatenate([jnp.where(low, vd, 0.0), jnp.where(low, 0.0, vd)], axis=0).astype(BF16)
        base = kv * npair * LANES
        qs = jnp.concatenate(
            [q_ref[:, base + p * LANES:base + (p + 1) * LANES] for p in range(npair)], axis=0)
        s = lax.dot_general(qs, kk, NT_DIMS, preferred_element_type=F32) + bias_ref[kv]
        s = jnp.where(no_prev, MASKED, s)
        es, ls = [], []
        for half in range(2):
            sh = s[:, half * 2 * w:(half + 1) * 2 * w]
            sink = sink_ref[kv, :, half:half + 1]
            m = jnp.maximum(jnp.max(sh, axis=1, keepdims=True), sink)
            e = jnp.exp(sh - m)
            es.append(e)
            ls.append(jnp.sum(e, axis=1, keepdims=True) + jnp.exp(sink - m))
        p = jnp.concatenate(es, axis=1).astype(BF16)
        o = jnp.dot(p, vv, preferred_element_type=F32)
        o = o / jnp.where(out_lane < SWA_HEAD_DIM, ls[0], ls[1])
        for pp in range(npair):
            o_ref[:, base + pp * LANES:base + (pp + 1) * LANES] = (
                o[pp * w:(pp + 1) * w, :].astype(o_ref.dtype))


def _swa_attention(q, kvdup, sinks, t):
    w = SWA_WINDOW
    dq = SWA_Q_HEADS * SWA_HEAD_DIM
    dkv = SWA_KV_HEADS * LANES
    npair = SWA_GROUP // 2
    sink_tab = jnp.broadcast_to(
        sinks.astype(F32).reshape(SWA_KV_HEADS, npair, 1, 2),
        (SWA_KV_HEADS, npair, w, 2)).reshape(SWA_KV_HEADS, npair * w, 2)
    prev = lambda n: jnp.maximum(n - 1, 0)
    return pl.pallas_call(
        _swa_kernel,
        grid=(t // w,),
        in_specs=[
            pl.BlockSpec((w, dq), lambda n: (n, 0)),
            pl.BlockSpec((w, dkv), lambda n: (prev(n), 0)),
            pl.BlockSpec((w, dkv), lambda n: (n, 0)),
            pl.BlockSpec((w, dkv), lambda n: (prev(n), 1)),
            pl.BlockSpec((w, dkv), lambda n: (n, 1)),
            pl.BlockSpec((SWA_KV_HEADS, npair * w, 4 * w), lambda n: (0, 0, 0)),
            pl.BlockSpec((SWA_KV_HEADS, npair * w, 2), lambda n: (0, 0, 0)),
        ],
        out_specs=pl.BlockSpec((w, dq), lambda n: (n, 0)),
        out_shape=jax.ShapeDtypeStruct((t, dq), BF16),
        compiler_params=_params(("arbitrary",), 48),
        name="swa_attention",
    )(q, kvdup, kvdup, kvdup, kvdup, _swa_bias_const(), sink_tab)


def kernel(x, c, norm_g, ada_w, ada_b, ffn_w_gate, ffn_w_up, ffn_w_down, moba_w_qkv, moba_w_o,
           pool_w, pool_scale, swa_w_qkv, swa_w_o, swa_sinks, final_g):
    b, t, d = x.shape
    assert b == 1
    depth = norm_g.shape[0]
    mods = _ada_mod(c, ada_w, ada_b)

    wg = ffn_w_gate.astype(BF16)
    wu = ffn_w_up.astype(BF16)
    wd = ffn_w_down.astype(BF16)
    moba_qkv_bf16 = moba_w_qkv.astype(BF16)
    moba_o_bf16 = moba_w_o.astype(BF16)
    swa_o_bf16 = swa_w_o.astype(BF16)

    def ffn(xs, i, s, mod3, final_norm):
        return _ffn(xs, norm_g[i, 2 * s], mod3, wg, wu, wd, i, s, final_g, final_norm=final_norm)

    xs = x.reshape(t, d)
    for i in range(depth):
        mod = mods[i]
        last = i == depth - 1
        xs = ffn(xs, i, 0, mod[0:3], False)
        kind, j = i % 3, i // 3
        m2 = mod[3:6]
        gate2 = mod[5:6]
        if kind == 0:
            qscale = jnp.concatenate([jnp.full((d,), MOBA_HEAD_DIM ** -0.5 * LOG2E, F32), jnp.ones((2 * d,), F32)])
            qkv = _norm_mod_matmul(xs, norm_g[i, 1], m2, moba_qkv_bf16, j, qscale)
            o = _moba_attention(qkv, t)
            xs = _proj_residual(o, moba_o_bf16, j, xs, gate2)
        elif kind == 1:
            xs = _pool_mixer(xs, norm_g[i, 1], m2, pool_w[j].astype(BF16), pool_scale[j])
        else:
            dq = SWA_Q_HEADS * SWA_HEAD_DIM
            hd = SWA_HEAD_DIM
            wqkv = swa_w_qkv[j]
            wq = wqkv[:, :dq]
            wk = wqkv[:, dq:dq + SWA_KV_HEADS * hd].reshape(d, SWA_KV_HEADS, 1, hd)
            wv = wqkv[:, dq + SWA_KV_HEADS * hd:].reshape(d, SWA_KV_HEADS, 1, hd)
            dup = lambda a: jnp.broadcast_to(a, (d, SWA_KV_HEADS, 2, hd)).reshape(d, SWA_KV_HEADS * 2 * hd)
            wkv = jnp.concatenate([dup(wk), dup(wv)], axis=1).astype(BF16)
            q = _norm_mod_matmul(xs, norm_g[i, 1], m2, wq.astype(BF16)[None], 0,
                                 jnp.full((dq,), hd ** -0.5, F32))
            kvdup = _norm_mod_matmul(xs, norm_g[i, 1], m2, wkv[None], 0, jnp.ones((wkv.shape[1],), F32))
            o = _swa_attention(q, kvdup, swa_sinks[j], t)
            xs = _proj_residual(o, swa_o_bf16, j, xs, gate2)
        xs = ffn(xs, i, 1, mod[6:9], last)
    return xs.reshape(b, t, d)
```

```python
import functools

import numpy as np
import jax
import jax.numpy as jnp
from jax import lax
from jax.experimental import pallas as pl
from jax.experimental.pallas import tpu as pltpu

F32 = jnp.float32
BF16 = jnp.bfloat16

NORM_EPS = 1e-6
FFN_RES = 0.5
MASKED = -1e30
LOG2E = 1.4426950408889634

LANES = 128
V7X_VMEM_BYTES = 64 * 1024 * 1024

MOBA_HEADS = 16
MOBA_HEAD_DIM = 128
MOBA_BLOCK = 256
MOBA_TOPK = 3
MOBA_QTILE = 2 * MOBA_BLOCK
MOBA_MAX_BLOCKS = 32

POOL_WINDOWS = (2, 4, 8, 16)
POOL_HALO = 16

SWA_HEAD_DIM = 64
SWA_Q_HEADS = 32
SWA_KV_HEADS = 4
SWA_GROUP = SWA_Q_HEADS // SWA_KV_HEADS
SWA_WINDOW = 128

NT_DIMS = (((1,), (1,)), ((), ()))


def _params(semantics, vmem_mib):
    return pltpu.CompilerParams(dimension_semantics=semantics,
                                vmem_limit_bytes=vmem_mib * 1024 * 1024)


def _norm_mod(xf, g, shift, scale):
    ms = jnp.mean(xf * xf, axis=-1, keepdims=True)
    y = xf * lax.rsqrt(ms + NORM_EPS)
    return (y * g) * (1.0 + scale) + shift


def _split_bf16(v):
    hi = v.astype(BF16).astype(F32)
    lo = (v - hi).astype(BF16).astype(F32)
    return hi, lo


def _ada_kernel(c_ref, w_ref, b_ref, o_ref):
    cb = c_ref[...]
    cs = cb * jax.nn.sigmoid(cb)
    w = w_ref[...]
    prod = w * jnp.concatenate([cs] * (w.shape[1] // LANES), axis=1)
    o_ref[...] = jnp.sum(prod, axis=0, keepdims=True) + b_ref[...]


def _ada_mod(c, ada_w, ada_b, tn=1024):
    depth, d, n = ada_w.shape
    c_rep = jnp.broadcast_to(c.reshape(d, 1), (d, LANES))
    out = pl.pallas_call(
        _ada_kernel,
        grid=(depth, n // tn),
        in_specs=[
            pl.BlockSpec((d, LANES), lambda l, j: (0, 0)),
            pl.BlockSpec((None, d, tn), lambda l, j: (l, 0, j)),
            pl.BlockSpec((None, 1, tn), lambda l, j: (l, 0, j)),
        ],
        out_specs=pl.BlockSpec((None, 1, tn), lambda l, j: (l, 0, j)),
        out_shape=jax.ShapeDtypeStruct((depth, 1, n), F32),
        compiler_params=_params(("arbitrary", "arbitrary"), 40),
        name="ada_mod",
    )(c_rep, ada_w, ada_b.reshape(depth, 1, n))
    return out.reshape(depth, n // d, d)


def _ffn_kernel(x_ref, g_ref, mod_ref, wg_ref, wu_ref, wd_ref, fg_ref, o_ref, h_ref, *, final_norm):
    j = pl.program_id(1)

    @pl.when(j == 0)
    def _():
        h = _norm_mod(x_ref[...], g_ref[...], mod_ref[0:1, :], mod_ref[1:2, :])
        h_ref[...] = h.astype(BF16)
        o_ref[...] = jnp.zeros(o_ref.shape, F32)

    h = h_ref[...]
    gg = jnp.dot(h, wg_ref[...].astype(BF16), preferred_element_type=F32)
    uu = jnp.dot(h, wu_ref[...].astype(BF16), preferred_element_type=F32)
    a = ((gg * jax.nn.sigmoid(gg)) * uu).astype(BF16)
    o_ref[...] += jnp.dot(a, wd_ref[...].astype(BF16), preferred_element_type=F32)

    @pl.when(j == pl.num_programs(1) - 1)
    def _():
        out = x_ref[...] + (FFN_RES * mod_ref[2:3, :]) * o_ref[...]
        if final_norm:
            ms = jnp.mean(out * out, axis=-1, keepdims=True)
            out = (out * lax.rsqrt(ms + NORM_EPS)) * fg_ref[...]
        o_ref[...] = out


def _ffn(x, g, mod3, wg, wu, wd, layer, slot, final_g, *, final_norm, tm=1024, tf=256):
    t, d = x.shape
    f = wg.shape[3]
    return pl.pallas_call(
        functools.partial(_ffn_kernel, final_norm=final_norm),
        grid=(t // tm, f // tf),
        in_specs=[
            pl.BlockSpec((tm, d), lambda i, j: (i, 0)),
            pl.BlockSpec((1, d), lambda i, j: (0, 0)),
            pl.BlockSpec((3, d), lambda i, j: (0, 0)),
            pl.BlockSpec((None, None, d, tf), lambda i, j: (layer, slot, 0, j)),
            pl.BlockSpec((None, None, d, tf), lambda i, j: (layer, slot, 0, j)),
            pl.BlockSpec((None, None, tf, d), lambda i, j: (layer, slot, j, 0)),
            pl.BlockSpec((1, d), lambda i, j: (0, 0)),
        ],
        out_specs=pl.BlockSpec((tm, d), lambda i, j: (i, 0)),
        out_shape=jax.ShapeDtypeStruct((t, d), F32),
        scratch_shapes=[pltpu.VMEM((tm, d), BF16)],
        compiler_params=_params(("arbitrary", "arbitrary"), 60),
        name="ffn",
    )(x, g.reshape(1, d), mod3, wg, wu, wd, final_g.reshape(1, d))


def _nmm_kernel(x_ref, g_ref, mod_ref, w_ref, cs_ref, o_ref, h_ref):
    @pl.when(pl.program_id(1) == 0)
    def _():
        h = _norm_mod(x_ref[...], g_ref[...], mod_ref[0:1, :], mod_ref[1:2, :])
        h_ref[...] = h.astype(BF16)

    r = jnp.dot(h_ref[...], w_ref[...], preferred_element_type=F32)
    o_ref[...] = (r * cs_ref[...]).astype(o_ref.dtype)


def _norm_mod_matmul(x, g, mod3, w, widx, colscale, *, tm=1024, tn=1024):
    t, d = x.shape
    n = w.shape[2]
    return pl.pallas_call(
        _nmm_kernel,
        grid=(t // tm, n // tn),
        in_specs=[
            pl.BlockSpec((tm, d), lambda i, j: (i, 0)),
            pl.BlockSpec((1, d), lambda i, j: (0, 0)),
            pl.BlockSpec((3, d), lambda i, j: (0, 0)),
            pl.BlockSpec((None, d, tn), lambda i, j: (widx, 0, j)),
            pl.BlockSpec((1, tn), lambda i, j: (0, j)),
        ],
        out_specs=pl.BlockSpec((tm, tn), lambda i, j: (i, j)),
        out_shape=jax.ShapeDtypeStruct((t, n), BF16),
        scratch_shapes=[pltpu.VMEM((tm, d), BF16)],
        compiler_params=_params(("arbitrary", "arbitrary"), 48),
        name="norm_mod_matmul",
    )(x, g.reshape(1, d), mod3, w, colscale.reshape(1, n))


def _proj_res_kernel(a_ref, w_ref, x_ref, gate_ref, o_ref):
    r = jnp.dot(a_ref[...], w_ref[...], preferred_element_type=F32)
    o_ref[...] = x_ref[...] + gate_ref[...] * r


def _proj_residual(a, w, widx, x, gate, *, tm=512):
    t, k = a.shape
    d = w.shape[2]
    return pl.pallas_call(
        _proj_res_kernel,
        grid=(t // tm,),
        in_specs=[
            pl.BlockSpec((tm, k), lambda i: (i, 0)),
            pl.BlockSpec((None, k, d), lambda i: (widx, 0, 0)),
            pl.BlockSpec((tm, d), lambda i: (i, 0)),
            pl.BlockSpec((1, d), lambda i: (0, 0)),
        ],
        out_specs=pl.BlockSpec((tm, d), lambda i: (i, 0)),
        out_shape=jax.ShapeDtypeStruct((t, d), F32),
        compiler_params=_params(("arbitrary",), 48),
        name="proj_residual",
    )(a, w, x, gate)


def _moba_key_const(t):
    pos = np.arange(t)
    lane = np.arange(LANES)[None, :]
    blk = (pos // MOBA_BLOCK)[:, None]
    inblk = (pos % MOBA_BLOCK)[:, None].astype(np.float32)
    kc = np.where(lane < 64, (blk == (lane % 32)).astype(np.float32), 0.0)
    kc = np.where((lane == 64) | (lane == 65), 1.0, kc)
    kc = np.where((lane == 66) | (lane == 67), inblk, kc)
    return jnp.asarray(kc, dtype=BF16)


def _alibi_slopes(n):
    return np.asarray([2.0 ** (-8.0 * (i + 1) / n) for i in range(n)], np.float32)


def _moba_kernel(slope_ref, q_ref, k_ref, v_ref, kc_ref, o_ref,
                 kaug, vaug, qaug, kmean_s, s_ring, mb_ring, acc_s, *, nblk):
    dh = MOBA_HEAD_DIM
    tq = MOBA_QTILE
    pair = pl.program_id(1)

    @pl.when(pair == 0)
    def _():
        k = k_ref[...]
        kaug[:, 0:dh] = k
        kaug[:, dh:2 * dh] = kc_ref[...]
        vaug[:, 0:dh] = v_ref[...]
        ones_col = lax.broadcasted_iota(jnp.int32, (vaug.shape[0], dh), 1) == 0
        vaug[:, dh:2 * dh] = jnp.where(ones_col, 1.0, 0.0).astype(BF16)
        kf = k.astype(F32).reshape(nblk, MOBA_BLOCK, dh)
        kmean_s[...] = jnp.zeros(kmean_s.shape, F32)
        kmean_s[0:nblk, :] = jnp.sum(kf, axis=1) * (1.0 / MOBA_BLOCK)

    q = q_ref[...]
    km_hi, km_lo = _split_bf16(kmean_s[...])
    gate_t = (lax.dot_general(km_hi.astype(BF16), q, NT_DIMS, preferred_element_type=F32)
              + lax.dot_general(km_lo.astype(BF16), q, NT_DIMS, preferred_element_type=F32))
    blk_t = lax.broadcasted_iota(jnp.int32, (MOBA_MAX_BLOCKS, tq), 0)
    row_t = lax.broadcasted_iota(jnp.int32, (MOBA_MAX_BLOCKS, tq), 1)
    qblk_t = 2 * pair + (row_t >= MOBA_BLOCK).astype(jnp.int32)
    blk_tf = blk_t.astype(F32)
    neg_inf = -jnp.inf

    g = jnp.where(blk_t < qblk_t, gate_t[0:MOBA_MAX_BLOCKS, :], neg_inf)
    act_t = jnp.where(blk_t == qblk_t, 1.0, 0.0)
    for _ in range(MOBA_TOPK):
        mx = jnp.max(g, axis=0, keepdims=True)
        idx = jnp.min(jnp.where(g == mx, blk_tf, float(LANES)), axis=0, keepdims=True)
        hit = blk_tf == jnp.where(mx > neg_inf, idx, float(LANES))
        act_t = jnp.where(hit, 1.0, act_t)
        g = jnp.where(hit, neg_inf, g)
    pad = jnp.zeros((LANES - 2 * MOBA_MAX_BLOCKS, tq), F32)
    active = jnp.concatenate([act_t, act_t, pad], axis=0).T > 0.5

    lane = lax.broadcasted_iota(jnp.int32, (tq, LANES), 1)
    row = lax.broadcasted_iota(jnp.int32, (tq, LANES), 0)
    blk = lane & (MOBA_MAX_BLOCKS - 1)
    slope = slope_ref[...]
    offset = -slope * (tq * pair - MOBA_BLOCK * blk).astype(F32)
    off_hi, off_lo = _split_bf16(offset)
    row_hi, row_lo = _split_bf16(-slope * row.astype(F32))
    sl_hi, sl_lo = _split_bf16(jnp.broadcast_to(slope, (tq, LANES)))
    extra = jnp.where(lane < 32, jnp.where(active, off_hi, MASKED),
            jnp.where(lane < 64, jnp.where(active, off_lo, 0.0),
            jnp.where(lane == 64, row_hi,
            jnp.where(lane == 65, row_lo,
            jnp.where(lane == 66, sl_hi,
            jnp.where(lane == 67, sl_lo, 0.0))))))
    qaug[:, 0:dh] = q
    qaug[:, dh:2 * dh] = extra.astype(BF16)

    def key_tile(u):
        return jnp.where(u == 0, pair, jnp.maximum(jnp.minimum(u, pair) - 1, 0))

    def scores(u):
        tc = pl.multiple_of(key_tile(u) * tq, tq)
        return lax.dot_general(qaug[...], kaug[pl.ds(tc, tq), :], NT_DIMS,
                               preferred_element_type=F32)

    def running_max(s, prev):
        r = jnp.max(s, axis=1, keepdims=True)
        return jnp.maximum(prev, jnp.broadcast_to(r, (tq, LANES)))

    r2 = lax.broadcasted_iota(jnp.int32, (tq, tq), 0)
    c2 = lax.broadcasted_iota(jnp.int32, (tq, tq), 1)
    s_own = jnp.where(c2 <= r2, scores(0), MASKED)
    floor = jnp.full((tq, LANES), MASKED, F32)
    s_ring[0] = s_own
    mb_ring[0] = running_max(s_own, floor)
    mb_ring[1] = floor
    s_ring[1] = scores(1)
    acc_s[...] = jnp.zeros(acc_s.shape, F32)

    def step(c, carry):
        cur, nxt, nxt2 = lax.rem(c, 3), lax.rem(c + 1, 3), lax.rem(c + 2, 3)
        mcur, mprev = lax.rem(c, 2), lax.rem(c + 1, 2)
        mb = mb_ring[mcur]
        alpha = jnp.exp2(mb_ring[mprev] - mb)
        p = jnp.exp2(s_ring[cur] - jnp.concatenate([mb] * (tq // LANES), axis=1)).astype(BF16)
        tv = pl.multiple_of(key_tile(c) * tq, tq)
        pv = jnp.dot(p, vaug[pl.ds(tv, tq), :], preferred_element_type=F32)
        acc_s[...] = acc_s[...] * jnp.concatenate([alpha, alpha], axis=1) + pv
        mb_ring[mprev] = running_max(s_ring[nxt], mb)
        s_ring[nxt2] = scores(c + 2)
        return carry

    lax.fori_loop(0, pair + 1, step, 0)

    acc = acc_s[...]
    o_ref[...] = (acc[:, 0:dh] / acc[:, dh:dh + 1]).astype(o_ref.dtype)


def _moba_attention(qkv, t):
    h, dh, tq = MOBA_HEADS, MOBA_HEAD_DIM, MOBA_QTILE
    nblk = t // MOBA_BLOCK
    assert t % tq == 0 and nblk <= MOBA_MAX_BLOCKS
    slopes = _alibi_slopes(h).astype(np.float64) * LOG2E
    slopes = jnp.asarray(np.broadcast_to(slopes.astype(np.float32)[:, None, None], (h, 1, LANES)))
    return pl.pallas_call(
        functools.partial(_moba_kernel, nblk=nblk),
        grid=(h, t // tq),
        in_specs=[
            pl.BlockSpec((None, 1, LANES), lambda hh, p: (hh, 0, 0)),
            pl.BlockSpec((tq, dh), lambda hh, p: (p, hh)),
            pl.BlockSpec((t, dh), lambda hh, p: (0, h + hh)),
            pl.BlockSpec((t, dh), lambda hh, p: (0, 2 * h + hh)),
            pl.BlockSpec((t, LANES), lambda hh, p: (0, 0)),
        ],
        out_specs=pl.BlockSpec((tq, dh), lambda hh, p: (p, hh)),
        out_shape=jax.ShapeDtypeStruct((t, h * dh), BF16),
        scratch_shapes=[
            pltpu.VMEM((t, 2 * dh), BF16),
            pltpu.VMEM((t, 2 * dh), BF16),
            pltpu.VMEM((tq, 2 * dh), BF16),
            pltpu.VMEM((LANES, dh), F32),
            pltpu.VMEM((3, tq, tq), F32),
            pltpu.VMEM((2, tq, LANES), F32),
            pltpu.VMEM((tq, 2 * dh), F32),
        ],
        compiler_params=_params(("arbitrary", "arbitrary"), 48),
        name="moba_attention",
    )(slopes, qkv, qkv, qkv, _moba_key_const(t))


def _pool_kernel(x_ref, xh_ref, g_ref, mod_ref, w_ref, ps_ref, o_ref, hbuf):
    i = pl.program_id(0)
    tm = x_ref.shape[0]
    grp = w_ref.shape[1]
    g, shift, scale, gate = g_ref[...], mod_ref[0:1, :], mod_ref[1:2, :], mod_ref[2:3, :]
    x = x_ref[...]
    h = _norm_mod(x, g, shift, scale)
    halo = _norm_mod(xh_ref[...], g, shift, scale)
    hbuf[0:POOL_HALO, :] = jnp.where(i == 0, 0.0, halo)
    hbuf[POOL_HALO:POOL_HALO + tm, :] = h
    tpos = i * tm + lax.broadcasted_iota(jnp.int32, (tm, 1), 0)
    for gi, win in enumerate(POOL_WINDOWS):
        cols = slice(gi * grp, (gi + 1) * grp)
        hg = h[:, cols]
        acc = hg
        for back in range(1, win):
            acc = acc + hbuf[POOL_HALO - back:POOL_HALO - back + tm, cols]
        cnt = jnp.minimum(tpos + 1, win).astype(F32)
        pooled = acc / cnt - hg
        mixed = jnp.dot(pooled.astype(BF16), w_ref[gi], preferred_element_type=F32)
        o_ref[:, cols] = x[:, cols] + gate[:, cols] * (mixed * ps_ref[:, cols])


def _pool_mixer(x, g, mod3, w_pool, pool_scale, *, tm=512):
    t, d = x.shape
    ng, grp, _ = w_pool.shape
    halo_blocks = tm // POOL_HALO
    return pl.pallas_call(
        _pool_kernel,
        grid=(t // tm,),
        in_specs=[
            pl.BlockSpec((tm, d), lambda i: (i, 0)),
            pl.BlockSpec((POOL_HALO, d), lambda i: (jnp.maximum(i * halo_blocks - 1, 0), 0)),
            pl.BlockSpec((1, d), lambda i: (0, 0)),
            pl.BlockSpec((3, d), lambda i: (0, 0)),
            pl.BlockSpec((ng, grp, grp), lambda i: (0, 0, 0)),
            pl.BlockSpec((1, d), lambda i: (0, 0)),
        ],
        out_specs=pl.BlockSpec((tm, d), lambda i: (i, 0)),
        out_shape=jax.ShapeDtypeStruct((t, d), F32),
        scratch_shapes=[pltpu.VMEM((POOL_HALO + tm, d), F32)],
        compiler_params=_params(("arbitrary",), 48),
        name="pool_mixer",
    )(x, x, g.reshape(1, d), mod3, w_pool, pool_scale.reshape(1, d))


def _swa_bias_const():
    w = SWA_WINDOW
    r = np.arange(w)[:, None]
    c = np.arange(2 * w)[None, :]
    dist = (r + w - c).astype(np.float32)
    valid = (dist >= 0) & (dist < w)
    slopes = _alibi_slopes(SWA_Q_HEADS)
    npair = SWA_GROUP // 2
    out = np.empty((SWA_KV_HEADS, npair * w, 2 * 2 * w), np.float32)
    for kv in range(SWA_KV_HEADS):
        for p in range(npair):
            for half in range(2):
                sl = slopes[kv * SWA_GROUP + 2 * p + half]
                tile = np.where(valid, -sl * dist, np.float32(MASKED)).astype(np.float32)
                out[kv, p * w:(p + 1) * w, half * 2 * w:(half + 1) * 2 * w] = tile
    return jnp.asarray(out)


def _swa_kernel(q_ref, kp_ref, kc_ref, vp_ref, vc_ref, bias_ref, sink_ref, o_ref):
    n = pl.program_id(0)
    w = SWA_WINDOW
    npair = SWA_GROUP // 2
    lane = lax.broadcasted_iota(jnp.int32, (2 * w, LANES), 1)
    low = lane < SWA_HEAD_DIM
    col = lax.broadcasted_iota(jnp.int32, (npair * w, 4 * w), 1)
    prev_cols = (col & (2 * w - 1)) < w
    no_prev = jnp.logical_and(n == 0, prev_cols)
    out_lane = lax.broadcasted_iota(jnp.int32, (npair * w, LANES), 1)
    for kv in range(SWA_KV_HEADS):
        ksl = slice(kv * LANES, (kv + 1) * LANES)
        kd = jnp.concatenate([kp_ref[:, ksl], kc_ref[:, ksl]], axis=0).astype(F32)
        vd = jnp.concatenate([vp_ref[:, ksl], vc_ref[:, ksl]], axis=0).astype(F32)
        kk = jnp.concatenate([jnp.where(low, kd, 0.0), jnp.where(low, 0.0, kd)], axis=0).astype(BF16)
        vv = jnp.concatenate([jnp.where(low, vd, 0.0), jnp.where(low, 0.0, vd)], axis=0).astype(BF16)
        base = kv * npair * LANES
        qs = jnp.concatenate(
            [q_ref[:, base + p * LANES:base + (p + 1) * LANES] for p in range(npair)], axis=0)
        s = lax.dot_general(qs, kk, NT_DIMS, preferred_element_type=F32) + bias_ref[kv]
        s = jnp.where(no_prev, MASKED, s)
        es, ls = [], []
        for half in range(2):
            sh = s[:, half * 2 * w:(half + 1) * 2 * w]
            sink = sink_ref[kv, :, half:half + 1]
            m = jnp.maximum(jnp.max(sh, axis=1, keepdims=True), sink)
            e = jnp.exp(sh - m)
            es.append(e)
            ls.append(jnp.sum(e, axis=1, keepdims=True) + jnp.exp(sink - m))
        p = jnp.concatenate(es, axis=1).astype(BF16)
        o = jnp.dot(p, vv, preferred_element_type=F32)
        o = o / jnp.where(out_lane < SWA_HEAD_DIM, ls[0], ls[1])
        for pp in range(npair):
            o_ref[:, base + pp * LANES:base + (pp + 1) * LANES] = (
                o[pp * w:(pp + 1) * w, :].astype(o_ref.dtype))


def _swa_attention(q, kvdup, sinks, t):
    w = SWA_WINDOW
    dq = SWA_Q_HEADS * SWA_HEAD_DIM
    dkv = SWA_KV_HEADS * LANES
    npair = SWA_GROUP // 2
    sink_tab = jnp.broadcast_to(
        sinks.astype(F32).reshape(SWA_KV_HEADS, npair, 1, 2),
        (SWA_KV_HEADS, npair, w, 2)).reshape(SWA_KV_HEADS, npair * w, 2)
    prev = lambda n: jnp.maximum(n - 1, 0)
    return pl.pallas_call(
        _swa_kernel,
        grid=(t // w,),
        in_specs=[
            pl.BlockSpec((w, dq), lambda n: (n, 0)),
            pl.BlockSpec((w, dkv), lambda n: (prev(n), 0)),
            pl.BlockSpec((w, dkv), lambda n: (n, 0)),
            pl.BlockSpec((w, dkv), lambda n: (prev(n), 1)),
            pl.BlockSpec((w, dkv), lambda n: (n, 1)),
            pl.BlockSpec((SWA_KV_HEADS, npair * w, 4 * w), lambda n: (0, 0, 0)),
            pl.BlockSpec((SWA_KV_HEADS, npair * w, 2), lambda n: (0, 0, 0)),
        ],
        out_specs=pl.BlockSpec((w, dq), lambda n: (n, 0)),
        out_shape=jax.ShapeDtypeStruct((t, dq), BF16),
        compiler_params=_params(("arbitrary",), 48),
        name="swa_attention",
    )(q, kvdup, kvdup, kvdup, kvdup, _swa_bias_const(), sink_tab)


def kernel(x, c, norm_g, ada_w, ada_b, ffn_w_gate, ffn_w_up, ffn_w_down, moba_w_qkv, moba_w_o,
           pool_w, pool_scale, swa_w_qkv, swa_w_o, swa_sinks, final_g):
    b, t, d = x.shape
    assert b == 1
    depth = norm_g.shape[0]
    mods = _ada_mod(c, ada_w, ada_b)

    wg, wu, wd = ffn_w_gate, ffn_w_up, ffn_w_down
    moba_qkv_bf16 = moba_w_qkv.astype(BF16)
    moba_o_bf16 = moba_w_o.astype(BF16)
    swa_o_bf16 = swa_w_o.astype(BF16)

    def ffn(xs, i, s, mod3, final_norm):
        return _ffn(xs, norm_g[i, 2 * s], mod3, wg, wu, wd, i, s, final_g, final_norm=final_norm)

    xs = x.reshape(t, d)
    for i in range(depth):
        mod = mods[i]
        last = i == depth - 1
        xs = ffn(xs, i, 0, mod[0:3], False)
        kind, j = i % 3, i // 3
        m2 = mod[3:6]
        gate2 = mod[5:6]
        if kind == 0:
            qscale = jnp.concatenate([jnp.full((d,), MOBA_HEAD_DIM ** -0.5 * LOG2E, F32), jnp.ones((2 * d,), F32)])
            qkv = _norm_mod_matmul(xs, norm_g[i, 1], m2, moba_qkv_bf16, j, qscale)
            o = _moba_attention(qkv, t)
            xs = _proj_residual(o, moba_o_bf16, j, xs, gate2)
        elif kind == 1:
            xs = _pool_mixer(xs, norm_g[i, 1], m2, pool_w[j].astype(BF16), pool_scale[j])
        else:
            dq = SWA_Q_HEADS * SWA_HEAD_DIM
            hd = SWA_HEAD_DIM
            wqkv = swa_w_qkv[j]
            wq = wqkv[:, :dq]
            wk = wqkv[:, dq:dq + SWA_KV_HEADS * hd].reshape(d, SWA_KV_HEADS, 1, hd)
            wv = wqkv[:, dq + SWA_KV_HEADS * hd:].reshape(d, SWA_KV_HEADS, 1, hd)
            dup = lambda a: jnp.broadcast_to(a, (d, SWA_KV_HEADS, 2, hd)).reshape(d, SWA_KV_HEADS * 2 * hd)
            wkv = jnp.concatenate([dup(wk), dup(wv)], axis=1).astype(BF16)
            q = _norm_mod_matmul(xs, norm_g[i, 1], m2, wq.astype(BF16)[None], 0,
                                 jnp.full((dq,), hd ** -0.5, F32))
            kvdup = _norm_mod_matmul(xs, norm_g[i, 1], m2, wkv[None], 0, jnp.ones((wkv.shape[1],), F32))
            o = _swa_attention(q, kvdup, swa_sinks[j], t)
            xs = _proj_residual(o, swa_o_bf16, j, xs, gate2)
        xs = ffn(xs, i, 1, mod[6:9], last)
    return xs.reshape(b, t, d)
```

```python
import functools

import numpy as np
import jax
import jax.numpy as jnp
from jax import lax
from jax.experimental import pallas as pl
from jax.experimental.pallas import tpu as pltpu

F32 = jnp.float32
BF16 = jnp.bfloat16

NORM_EPS = 1e-6
FFN_RES = 0.5
MASKED = -1e30
MAX_FLOOR = -1e20
LOG2E = 1.4426950408889634

LANES = 128
V7X_VMEM_BYTES = 64 * 1024 * 1024

MOBA_HEADS = 16
MOBA_HEAD_DIM = 128
MOBA_BLOCK = 256
MOBA_TOPK = 3
MOBA_QTILE = 2 * MOBA_BLOCK
MOBA_KTILE = 2 * MOBA_BLOCK
MOBA_MAX_BLOCKS = 32

POOL_WINDOWS = (2, 4, 8, 16)
POOL_HALO = 16

SWA_HEAD_DIM = 64
SWA_Q_HEADS = 32
SWA_KV_HEADS = 4
SWA_GROUP = SWA_Q_HEADS // SWA_KV_HEADS
SWA_WINDOW = 128

NT_DIMS = (((1,), (1,)), ((), ()))


def _params(semantics, vmem_mib):
    return pltpu.CompilerParams(dimension_semantics=semantics,
                                vmem_limit_bytes=vmem_mib * 1024 * 1024)


def _norm_mod(xf, g, shift, scale):
    ms = jnp.mean(xf * xf, axis=-1, keepdims=True)
    y = xf * lax.rsqrt(ms + NORM_EPS)
    return y * (g * (1.0 + scale)) + shift


def _split_bf16(v):
    hi = v.astype(BF16).astype(F32)
    lo = (v - hi).astype(BF16).astype(F32)
    return hi, lo


def _ada_kernel(c_ref, w_ref, b_ref, o_ref):
    cb = c_ref[...]
    cs = cb * jax.nn.sigmoid(cb)
    w = w_ref[...]
    prod = w * jnp.concatenate([cs] * (w.shape[1] // LANES), axis=1)
    o_ref[...] = jnp.sum(prod, axis=0, keepdims=True) + b_ref[...]


def _ada_mod(c, ada_w, ada_b, tn=1024):
    depth, d, n = ada_w.shape
    c_rep = jnp.broadcast_to(c.reshape(d, 1), (d, LANES))
    out = pl.pallas_call(
        _ada_kernel,
        grid=(depth, n // tn),
        in_specs=[
            pl.BlockSpec((d, LANES), lambda l, j: (0, 0)),
            pl.BlockSpec((None, d, tn), lambda l, j: (l, 0, j)),
            pl.BlockSpec((None, 1, tn), lambda l, j: (l, 0, j)),
        ],
        out_specs=pl.BlockSpec((None, 1, tn), lambda l, j: (l, 0, j)),
        out_shape=jax.ShapeDtypeStruct((depth, 1, n), F32),
        compiler_params=_params(("arbitrary", "arbitrary"), 40),
        name="ada_mod",
    )(c_rep, ada_w, ada_b.reshape(depth, 1, n))
    return out.reshape(depth, n // d, d)


def _ffn_kernel(x_ref, g_ref, mod_ref, wg_ref, wu_ref, wd_ref, fg_ref, o_ref, h_ref, *, final_norm):
    j = pl.program_id(1)
    last = pl.num_programs(1) - 1

    def partial_out(h):
        gg = jnp.dot(h, wg_ref[...].astype(BF16), preferred_element_type=F32)
        uu = jnp.dot(h, wu_ref[...].astype(BF16), preferred_element_type=F32)
        a = ((gg * jax.nn.sigmoid(gg)) * uu).astype(BF16)
        return jnp.dot(a, wd_ref[...].astype(BF16), preferred_element_type=F32)

    @pl.when(j == 0)
    def _():
        h = _norm_mod(x_ref[...], g_ref[...], mod_ref[0:1, :], mod_ref[1:2, :]).astype(BF16)
        h_ref[...] = h
        o_ref[...] = partial_out(h)

    @pl.when(jnp.logical_and(j > 0, j < last))
    def _():
        o_ref[...] += partial_out(h_ref[...])

    @pl.when(j == last)
    def _():
        acc = o_ref[...] + partial_out(h_ref[...])
        out = x_ref[...] + (FFN_RES * mod_ref[2:3, :]) * acc
        if final_norm:
            ms = jnp.mean(out * out, axis=-1, keepdims=True)
            out = (out * lax.rsqrt(ms + NORM_EPS)) * fg_ref[...]
        o_ref[...] = out


def _ffn(x, g, mod3, wg, wu, wd, layer, slot, final_g, *, final_norm, tm=1024, tf=256):
    t, d = x.shape
    f = wg.shape[3]
    return pl.pallas_call(
        functools.partial(_ffn_kernel, final_norm=final_norm),
        grid=(t // tm, f // tf),
        in_specs=[
            pl.BlockSpec((tm, d), lambda i, j: (i, 0)),
            pl.BlockSpec((1, d), lambda i, j: (0, 0)),
            pl.BlockSpec((3, d), lambda i, j: (0, 0)),
            pl.BlockSpec((None, None, d, tf), lambda i, j: (layer, slot, 0, j)),
            pl.BlockSpec((None, None, d, tf), lambda i, j: (layer, slot, 0, j)),
            pl.BlockSpec((None, None, tf, d), lambda i, j: (layer, slot, j, 0)),
            pl.BlockSpec((1, d), lambda i, j: (0, 0)),
        ],
        out_specs=pl.BlockSpec((tm, d), lambda i, j: (i, 0)),
        out_shape=jax.ShapeDtypeStruct((t, d), F32),
        scratch_shapes=[pltpu.VMEM((tm, d), BF16)],
        compiler_params=_params(("arbitrary", "arbitrary"), 60),
        name="ffn",
    )(x, g.reshape(1, d), mod3, wg, wu, wd, final_g.reshape(1, d))


def _nmm_kernel(x_ref, g_ref, mod_ref, w_ref, cs_ref, o_ref, h_ref):
    def project(h):
        r = jnp.dot(h, w_ref[...].astype(BF16), preferred_element_type=F32)
        o_ref[...] = (r * cs_ref[...]).astype(o_ref.dtype)

    @pl.when(pl.program_id(1) == 0)
    def _():
        h = _norm_mod(x_ref[...], g_ref[...], mod_ref[0:1, :], mod_ref[1:2, :]).astype(BF16)
        h_ref[...] = h
        project(h)

    @pl.when(pl.program_id(1) > 0)
    def _():
        project(h_ref[...])


def _norm_mod_matmul(x, g, mod3, w, widx, colscale, *, tm=1024, tn=1024):
    t, d = x.shape
    n = w.shape[2]
    return pl.pallas_call(
        _nmm_kernel,
        grid=(t // tm, n // tn),
        in_specs=[
            pl.BlockSpec((tm, d), lambda i, j: (i, 0)),
            pl.BlockSpec((1, d), lambda i, j: (0, 0)),
            pl.BlockSpec((3, d), lambda i, j: (0, 0)),
            pl.BlockSpec((None, d, tn), lambda i, j: (widx, 0, j)),
            pl.BlockSpec((1, tn), lambda i, j: (0, j)),
        ],
        out_specs=pl.BlockSpec((tm, tn), lambda i, j: (i, j)),
        out_shape=jax.ShapeDtypeStruct((t, n), BF16),
        scratch_shapes=[pltpu.VMEM((tm, d), BF16)],
        compiler_params=_params(("arbitrary", "arbitrary"), 48),
        name="norm_mod_matmul",
    )(x, g.reshape(1, d), mod3, w, colscale.reshape(1, n))


def _proj_res_kernel(a_ref, w_ref, x_ref, gate_ref, o_ref):
    r = jnp.dot(a_ref[...], w_ref[...], preferred_element_type=F32)
    o_ref[...] = x_ref[...] + gate_ref[...] * r


def _proj_residual(a, w, widx, x, gate, *, tm=512):
    t, k = a.shape
    d = w.shape[2]
    return pl.pallas_call(
        _proj_res_kernel,
        grid=(t // tm,),
        in_specs=[
            pl.BlockSpec((tm, k), lambda i: (i, 0)),
            pl.BlockSpec((None, k, d), lambda i: (widx, 0, 0)),
            pl.BlockSpec((tm, d), lambda i: (i, 0)),
            pl.BlockSpec((1, d), lambda i: (0, 0)),
        ],
        out_specs=pl.BlockSpec((tm, d), lambda i: (i, 0)),
        out_shape=jax.ShapeDtypeStruct((t, d), F32),
        compiler_params=_params(("arbitrary",), 48),
        name="proj_residual",
    )(a, w, x, gate)


def _moba_key_const(t):
    pos = np.arange(t)
    lane = np.arange(LANES)[None, :]
    blk = (pos // MOBA_BLOCK)[:, None]
    inblk = (pos % MOBA_BLOCK)[:, None].astype(np.float32)
    kc = np.where(lane < 64, (blk == (lane % 32)).astype(np.float32), 0.0)
    kc = np.where((lane == 64) | (lane == 65), 1.0, kc)
    kc = np.where((lane == 66) | (lane == 67), inblk, kc)
    return jnp.asarray(kc, dtype=BF16)


def _alibi_slopes(n):
    return np.asarray([2.0 ** (-8.0 * (i + 1) / n) for i in range(n)], np.float32)


def _moba_kernel(slope_ref, q_ref, k_ref, v_ref, kc_ref, o_ref,
                 kaug, vaug, qaug, kmean_s, s_ring, mb_ring, acc_s, *, nblk):
    dh = MOBA_HEAD_DIM
    tq, tk = MOBA_QTILE, MOBA_KTILE
    own_tiles = tq // tk
    qt = pl.program_id(1)
    n_past = own_tiles * qt

    @pl.when(qt == 0)
    def _():
        k = k_ref[...]
        kaug[:, 0:dh] = k
        kaug[:, dh:2 * dh] = kc_ref[...]
        vaug[:, 0:dh] = v_ref[...]
        ones_col = lax.broadcasted_iota(jnp.int32, (vaug.shape[0], dh), 1) == 0
        vaug[:, dh:2 * dh] = jnp.where(ones_col, 1.0, 0.0).astype(BF16)
        kf = k.astype(F32).reshape(nblk, MOBA_BLOCK, dh)
        kmean_s[...] = jnp.zeros(kmean_s.shape, F32)
        kmean_s[0:nblk, :] = jnp.sum(kf, axis=1) * (1.0 / MOBA_BLOCK)

    q = q_ref[...]
    km_hi, km_lo = _split_bf16(kmean_s[...])
    gate_t = (lax.dot_general(km_hi.astype(BF16), q, NT_DIMS, preferred_element_type=F32)
              + lax.dot_general(km_lo.astype(BF16), q, NT_DIMS, preferred_element_type=F32))
    blk_t = lax.broadcasted_iota(jnp.int32, (MOBA_MAX_BLOCKS, tq), 0)
    row_t = lax.broadcasted_iota(jnp.int32, (MOBA_MAX_BLOCKS, tq), 1)
    qblk_t = (tq // MOBA_BLOCK) * qt + row_t // MOBA_BLOCK
    blk_tf = blk_t.astype(F32)
    neg_inf = -jnp.inf

    g = jnp.where(blk_t < qblk_t, gate_t[0:MOBA_MAX_BLOCKS, :], neg_inf)
    act_t = jnp.where(blk_t == qblk_t, 1.0, 0.0)
    for _ in range(MOBA_TOPK):
        mx = jnp.max(g, axis=0, keepdims=True)
        idx = jnp.min(jnp.where(g == mx, blk_tf, float(LANES)), axis=0, keepdims=True)
        hit = blk_tf == jnp.where(mx > neg_inf, idx, float(LANES))
        act_t = jnp.where(hit, 1.0, act_t)
        g = jnp.where(hit, neg_inf, g)
    pad = jnp.zeros((LANES - 2 * MOBA_MAX_BLOCKS, tq), F32)
    active = jnp.concatenate([act_t, act_t, pad], axis=0).T > 0.5

    lane = lax.broadcasted_iota(jnp.int32, (tq, LANES), 1)
    row = lax.broadcasted_iota(jnp.int32, (tq, LANES), 0)
    blk = lane & (MOBA_MAX_BLOCKS - 1)
    slope = slope_ref[...]
    offset = -slope * (tq * qt - MOBA_BLOCK * blk).astype(F32)
    off_hi, off_lo = _split_bf16(offset)
    row_hi, row_lo = _split_bf16(-slope * row.astype(F32))
    sl_hi, sl_lo = _split_bf16(jnp.broadcast_to(slope, (tq, LANES)))
    extra = jnp.where(lane < 32, jnp.where(active, off_hi, MASKED),
            jnp.where(lane < 64, jnp.where(active, off_lo, 0.0),
            jnp.where(lane == 64, row_hi,
            jnp.where(lane == 65, row_lo,
            jnp.where(lane == 66, sl_hi,
            jnp.where(lane == 67, sl_lo, 0.0))))))
    qaug[:, 0:dh] = q
    qaug[:, dh:2 * dh] = extra.astype(BF16)

    def key_tile(u):
        if own_tiles == 1:
            return jnp.where(u == 0, qt, jnp.maximum(jnp.minimum(u, qt) - 1, 0))
        past = jnp.maximum(jnp.minimum(u, n_past + own_tiles - 1) - own_tiles, 0)
        return jnp.where(u < own_tiles, n_past + u, past)

    def scores(u):
        tc = pl.multiple_of(key_tile(u) * tk, tk)
        return lax.dot_general(qaug[...], kaug[pl.ds(tc, tk), :], NT_DIMS,
                               preferred_element_type=F32)

    def own_scores(u):
        r2 = lax.broadcasted_iota(jnp.int32, (tq, tk), 0)
        c2 = lax.broadcasted_iota(jnp.int32, (tq, tk), 1)
        return jnp.where(c2 + u * tk <= r2, scores(u), MASKED)

    def running_max(s, prev):
        r = jnp.max(s, axis=1, keepdims=True)
        return jnp.maximum(prev, jnp.broadcast_to(r, (tq, LANES)))

    assert own_tiles in (1, 2)
    floor = jnp.full((tq, LANES), MAX_FLOOR, F32)
    s0 = own_scores(0)
    s_ring[0] = s0
    mb_ring[0] = running_max(s0, floor)
    mb_ring[1] = floor
    s_ring[1] = own_scores(1) if own_tiles == 2 else scores(1)
    acc_s[...] = jnp.zeros(acc_s.shape, F32)

    def step(c, carry):
        cur, nxt, nxt2 = lax.rem(c, 3), lax.rem(c + 1, 3), lax.rem(c + 2, 3)
        mcur, mprev = lax.rem(c, 2), lax.rem(c + 1, 2)
        mb = mb_ring[mcur]
        alpha = jnp.exp2(mb_ring[mprev] - mb)
        p = jnp.exp2(s_ring[cur] - jnp.concatenate([mb] * (tk // LANES), axis=1)).astype(BF16)
        tv = pl.multiple_of(key_tile(c) * tk, tk)
        pv = jnp.dot(p, vaug[pl.ds(tv, tk), :], preferred_element_type=F32)
        acc_s[...] = acc_s[...] * jnp.concatenate([alpha, alpha], axis=1) + pv
        mb_ring[mprev] = running_max(s_ring[nxt], mb)
        s_ring[nxt2] = scores(c + 2)
        return carry

    lax.fori_loop(0, n_past + own_tiles, step, 0)

    acc = acc_s[...]
    o_ref[...] = (acc[:, 0:dh] / acc[:, dh:dh + 1]).astype(o_ref.dtype)


def _moba_attention(qkv, t):
    h, dh, tq, tk = MOBA_HEADS, MOBA_HEAD_DIM, MOBA_QTILE, MOBA_KTILE
    nblk = t // MOBA_BLOCK
    assert t % tq == 0 and nblk <= MOBA_MAX_BLOCKS
    slopes = _alibi_slopes(h).astype(np.float64) * LOG2E
    slopes = jnp.asarray(np.broadcast_to(slopes.astype(np.float32)[:, None, None], (h, 1, LANES)))
    return pl.pallas_call(
        functools.partial(_moba_kernel, nblk=nblk),
        grid=(h, t // tq),
        in_specs=[
            pl.BlockSpec((None, 1, LANES), lambda hh, p: (hh, 0, 0)),
            pl.BlockSpec((tq, dh), lambda hh, p: (p, hh)),
            pl.BlockSpec((t, dh), lambda hh, p: (0, h + hh)),
            pl.BlockSpec((t, dh), lambda hh, p: (0, 2 * h + hh)),
            pl.BlockSpec((t, LANES), lambda hh, p: (0, 0)),
        ],
        out_specs=pl.BlockSpec((tq, dh), lambda hh, p: (p, hh)),
        out_shape=jax.ShapeDtypeStruct((t, h * dh), BF16),
        scratch_shapes=[
            pltpu.VMEM((t, 2 * dh), BF16),
            pltpu.VMEM((t, 2 * dh), BF16),
            pltpu.VMEM((tq, 2 * dh), BF16),
            pltpu.VMEM((LANES, dh), F32),
            pltpu.VMEM((3, tq, tk), F32),
            pltpu.VMEM((2, tq, LANES), F32),
            pltpu.VMEM((tq, 2 * dh), F32),
        ],
        compiler_params=_params(("arbitrary", "arbitrary"), 48),
        name="moba_attention",
    )(slopes, qkv, qkv, qkv, _moba_key_const(t))


def _pool_kernel(x_ref, xh_ref, g_ref, mod_ref, w_ref, ps_ref, o_ref, hbuf):
    i = pl.program_id(0)
    tm = x_ref.shape[0]
    grp = w_ref.shape[1]
    g, shift, scale, gate = g_ref[...], mod_ref[0:1, :], mod_ref[1:2, :], mod_ref[2:3, :]
    x = x_ref[...]
    h = _norm_mod(x, g, shift, scale)
    halo = _norm_mod(xh_ref[...], g, shift, scale)
    hbuf[0:POOL_HALO, :] = jnp.where(i == 0, 0.0, halo)
    hbuf[POOL_HALO:POOL_HALO + tm, :] = h
    tpos = i * tm + lax.broadcasted_iota(jnp.int32, (tm, 1), 0)
    for gi, win in enumerate(POOL_WINDOWS):
        cols = slice(gi * grp, (gi + 1) * grp)
        hg = h[:, cols]
        acc = hg
        for back in range(1, win):
            acc = acc + hbuf[POOL_HALO - back:POOL_HALO - back + tm, cols]
        cnt = jnp.minimum(tpos + 1, win).astype(F32)
        pooled = acc / cnt - hg
        mixed = jnp.dot(pooled.astype(BF16), w_ref[gi], preferred_element_type=F32)
        o_ref[:, cols] = x[:, cols] + gate[:, cols] * (mixed * ps_ref[:, cols])


def _pool_mixer(x, g, mod3, w_pool, pool_scale, *, tm=512):
    t, d = x.shape
    ng, grp, _ = w_pool.shape
    halo_blocks = tm // POOL_HALO
    return pl.pallas_call(
        _pool_kernel,
        grid=(t // tm,),
        in_specs=[
            pl.BlockSpec((tm, d), lambda i: (i, 0)),
            pl.BlockSpec((POOL_HALO, d), lambda i: (jnp.maximum(i * halo_blocks - 1, 0), 0)),
            pl.BlockSpec((1, d), lambda i: (0, 0)),
            pl.BlockSpec((3, d), lambda i: (0, 0)),
            pl.BlockSpec((ng, grp, grp), lambda i: (0, 0, 0)),
            pl.BlockSpec((1, d), lambda i: (0, 0)),
        ],
        out_specs=pl.BlockSpec((tm, d), lambda i: (i, 0)),
        out_shape=jax.ShapeDtypeStruct((t, d), F32),
        scratch_shapes=[pltpu.VMEM((POOL_HALO + tm, d), F32)],
        compiler_params=_params(("arbitrary",), 48),
        name="pool_mixer",
    )(x, x, g.reshape(1, d), mod3, w_pool, pool_scale.reshape(1, d))


def _swa_bias_const():
    w = SWA_WINDOW
    r = np.arange(w)[:, None]
    c = np.arange(2 * w)[None, :]
    dist = (r + w - c).astype(np.float32)
    valid = (dist >= 0) & (dist < w)
    slopes = _alibi_slopes(SWA_Q_HEADS)
    npair = SWA_GROUP // 2
    out = np.empty((SWA_KV_HEADS, npair * w, 2 * 2 * w), np.float32)
    for kv in range(SWA_KV_HEADS):
        for p in range(npair):
            for half in range(2):
                sl = slopes[kv * SWA_GROUP + 2 * p + half]
                tile = np.where(valid, -sl * dist, np.float32(MASKED)).astype(np.float32)
                out[kv, p * w:(p + 1) * w, half * 2 * w:(half + 1) * 2 * w] = tile
    return jnp.asarray(out)


def _swa_kernel(q_ref, kp_ref, kc_ref, vp_ref, vc_ref, bias_ref, sink_ref, o_ref):
    n = pl.program_id(0)
    w = SWA_WINDOW
    npair = SWA_GROUP // 2
    lane = lax.broadcasted_iota(jnp.int32, (2 * w, LANES), 1)
    low = lane < SWA_HEAD_DIM
    col = lax.broadcasted_iota(jnp.int32, (npair * w, 4 * w), 1)
    prev_cols = (col & (2 * w - 1)) < w
    no_prev = jnp.logical_and(n == 0, prev_cols)
    out_lane = lax.broadcasted_iota(jnp.int32, (npair * w, LANES), 1)
    for kv in range(SWA_KV_HEADS):
        ksl = slice(kv * LANES, (kv + 1) * LANES)
        kd = jnp.concatenate([kp_ref[:, ksl], kc_ref[:, ksl]], axis=0).astype(F32)
        vd = jnp.concatenate([vp_ref[:, ksl], vc_ref[:, ksl]], axis=0).astype(F32)
        kk = jnp.concatenate([jnp.where(low, kd, 0.0), jnp.where(low, 0.0, kd)], axis=0).astype(BF16)
        vv = jnp.concatenate([jnp.where(low, vd, 0.0), jnp.where(low, 0.0, vd)], axis=0).astype(BF16)
        base = kv * npair * LANES
        qs = jnp.concatenate(
            [q_ref[:, base + p * LANES:base + (p + 1) * LANES] for p in range(npair)], axis=0)
        s = lax.dot_general(qs, kk, NT_DIMS, preferred_element_type=F32) + bias_ref[kv]
        s = jnp.where(no_prev, MASKED, s)
        es, ls = [], []
        for half in range(2):
            sh = s[:, half * 2 * w:(half + 1) * 2 * w]
            sink = sink_ref[kv, :, half:half + 1]
            m = jnp.maximum(jnp.max(sh, axis=1, keepdims=True), sink)
            e = jnp.exp(sh - m)
            es.append(e)
            ls.append(jnp.sum(e, axis=1, keepdims=True) + jnp.exp(sink - m))
        p = jnp.concatenate(es, axis=1).astype(BF16)
        o = jnp.dot(p, vv, preferred_element_type=F32)
        o = o / jnp.where(out_lane < SWA_HEAD_DIM, ls[0], ls[1])
        for pp in range(npair):
            o_ref[:, base + pp * LANES:base + (pp + 1) * LANES] = (
                o[pp * w:(pp + 1) * w, :].astype(o_ref.dtype))


def _swa_attention(q, kvdup, sinks, t):
    w = SWA_WINDOW
    dq = SWA_Q_HEADS * SWA_HEAD_DIM
    dkv = SWA_KV_HEADS * LANES
    npair = SWA_GROUP // 2
    sink_tab = jnp.broadcast_to(
        sinks.astype(F32).reshape(SWA_KV_HEADS, npair, 1, 2),
        (SWA_KV_HEADS, npair, w, 2)).reshape(SWA_KV_HEADS, npair * w, 2)
    prev = lambda n: jnp.maximum(n - 1, 0)
    return pl.pallas_call(
        _swa_kernel,
        grid=(t // w,),
        in_specs=[
            pl.BlockSpec((w, dq), lambda n: (n, 0)),
            pl.BlockSpec((w, dkv), lambda n: (prev(n), 0)),
            pl.BlockSpec((w, dkv), lambda n: (n, 0)),
            pl.BlockSpec((w, dkv), lambda n: (prev(n), 1)),
            pl.BlockSpec((w, dkv), lambda n: (n, 1)),
            pl.BlockSpec((SWA_KV_HEADS, npair * w, 4 * w), lambda n: (0, 0, 0)),
            pl.BlockSpec((SWA_KV_HEADS, npair * w, 2), lambda n: (0, 0, 0)),
        ],
        out_specs=pl.BlockSpec((w, dq), lambda n: (n, 0)),
        out_shape=jax.ShapeDtypeStruct((t, dq), BF16),
        compiler_params=_params(("arbitrary",), 48),
        name="swa_attention",
    )(q, kvdup, kvdup, kvdup, kvdup, _swa_bias_const(), sink_tab)


def kernel(x, c, norm_g, ada_w, ada_b, ffn_w_gate, ffn_w_up, ffn_w_down, moba_w_qkv, moba_w_o,
           pool_w, pool_scale, swa_w_qkv, swa_w_o, swa_sinks, final_g):
    b, t, d = x.shape
    assert b == 1
    depth = norm_g.shape[0]
    mods = _ada_mod(c, ada_w, ada_b)

    wg, wu, wd = ffn_w_gate, ffn_w_up, ffn_w_down
    moba_o_bf16 = moba_w_o.astype(BF16)
    swa_o_bf16 = swa_w_o.astype(BF16)

    def ffn(xs, i, s, mod3, final_norm):
        return _ffn(xs, norm_g[i, 2 * s], mod3, wg, wu, wd, i, s, final_g, final_norm=final_norm)

    xs = x.reshape(t, d)
    for i in range(depth):
        mod = mods[i]
        last = i == depth - 1
        xs = ffn(xs, i, 0, mod[0:3], False)
        kind, j = i % 3, i // 3
        m2 = mod[3:6]
        gate2 = mod[5:6]
        if kind == 0:
            qscale = jnp.concatenate([jnp.full((d,), MOBA_HEAD_DIM ** -0.5 * LOG2E, F32), jnp.ones((2 * d,), F32)])
            qkv = _norm_mod_matmul(xs, norm_g[i, 1], m2, moba_w_qkv, j, qscale)
            o = _moba_attention(qkv, t)
            xs = _proj_residual(o, moba_o_bf16, j, xs, gate2)
        elif kind == 1:
            xs = _pool_mixer(xs, norm_g[i, 1], m2, pool_w[j].astype(BF16), pool_scale[j])
        else:
            dq = SWA_Q_HEADS * SWA_HEAD_DIM
            hd = SWA_HEAD_DIM
            wqkv = swa_w_qkv[j]
            wq = wqkv[:, :dq]
            wk = wqkv[:, dq:dq + SWA_KV_HEADS * hd].reshape(d, SWA_KV_HEADS, 1, hd)
            wv = wqkv[:, dq + SWA_KV_HEADS * hd:].reshape(d, SWA_KV_HEADS, 1, hd)
            dup = lambda a: jnp.broadcast_to(a, (d, SWA_KV_HEADS, 2, hd)).reshape(d, SWA_KV_HEADS * 2 * hd)
            wkv = jnp.concatenate([dup(wk), dup(wv)], axis=1)
            q = _norm_mod_matmul(xs, norm_g[i, 1], m2, wq[None], 0,
                                 jnp.full((dq,), hd ** -0.5, F32))
            kvdup = _norm_mod_matmul(xs, norm_g[i, 1], m2, wkv[None], 0, jnp.ones((wkv.shape[1],), F32))
            o = _swa_attention(q, kvdup, swa_sinks[j], t)
            xs = _proj_residual(o, swa_o_bf16, j, xs, gate2)
        xs = ffn(xs, i, 1, mod[6:9], last)
    return xs.reshape(b, t, d)
```

```python
import functools

import numpy as np
import jax
import jax.numpy as jnp
from jax import lax
from jax.experimental import pallas as pl
from jax.experimental.pallas import tpu as pltpu

F32 = jnp.float32
BF16 = jnp.bfloat16

NORM_EPS = 1e-6
FFN_RES = 0.5
MASKED = -1e30
MAX_FLOOR = -1e20
LOG2E = 1.4426950408889634

LANES = 128
V7X_VMEM_BYTES = 64 * 1024 * 1024

MOBA_HEADS = 16
MOBA_HEAD_DIM = 128
MOBA_BLOCK = 256
MOBA_TOPK = 3
MOBA_QTILE = 2 * MOBA_BLOCK
MOBA_KTILE = 2 * MOBA_BLOCK
MOBA_MAX_BLOCKS = 32

POOL_WINDOWS = (2, 4, 8, 16)
POOL_HALO = 16

SWA_HEAD_DIM = 64
SWA_Q_HEADS = 32
SWA_KV_HEADS = 4
SWA_GROUP = SWA_Q_HEADS // SWA_KV_HEADS
SWA_WINDOW = 128

NT_DIMS = (((1,), (1,)), ((), ()))


def _params(semantics, vmem_mib):
    return pltpu.CompilerParams(dimension_semantics=semantics,
                                vmem_limit_bytes=vmem_mib * 1024 * 1024)


def _norm_mod(xf, g, shift, scale):
    ms = jnp.mean(xf * xf, axis=-1, keepdims=True)
    y = xf * lax.rsqrt(ms + NORM_EPS)
    return y * (g * (1.0 + scale)) + shift


def _split_bf16(v):
    hi = v.astype(BF16).astype(F32)
    lo = (v - hi).astype(BF16).astype(F32)
    return hi, lo


def _ada_kernel(c_ref, w_ref, b_ref, o_ref):
    cb = c_ref[...]
    cs = cb * jax.nn.sigmoid(cb)
    w = w_ref[...]
    prod = w * jnp.concatenate([cs] * (w.shape[1] // LANES), axis=1)
    o_ref[...] = jnp.sum(prod, axis=0, keepdims=True) + b_ref[...]


def _ada_mod(c, ada_w, ada_b, tn=1024):
    depth, d, n = ada_w.shape
    c_rep = jnp.broadcast_to(c.reshape(d, 1), (d, LANES))
    out = pl.pallas_call(
        _ada_kernel,
        grid=(depth, n // tn),
        in_specs=[
            pl.BlockSpec((d, LANES), lambda l, j: (0, 0)),
            pl.BlockSpec((None, d, tn), lambda l, j: (l, 0, j)),
            pl.BlockSpec((None, 1, tn), lambda l, j: (l, 0, j)),
        ],
        out_specs=pl.BlockSpec((None, 1, tn), lambda l, j: (l, 0, j)),
        out_shape=jax.ShapeDtypeStruct((depth, 1, n), F32),
        compiler_params=_params(("arbitrary", "arbitrary"), 40),
        name="ada_mod",
    )(c_rep, ada_w, ada_b.reshape(depth, 1, n))
    return out.reshape(depth, n // d, d)


def _ffn_kernel(x_ref, g_ref, mod_ref, wg_ref, wu_ref, wd_ref, fg_ref, o_ref, h_ref, *, final_norm):
    j = pl.program_id(1)
    last = pl.num_programs(1) - 1

    def partial_out(h):
        gg = jnp.dot(h, wg_ref[...].astype(BF16), preferred_element_type=F32)
        uu = jnp.dot(h, wu_ref[...].astype(BF16), preferred_element_type=F32)
        a = ((gg * jax.nn.sigmoid(gg)) * uu).astype(BF16)
        return jnp.dot(a, wd_ref[...].astype(BF16), preferred_element_type=F32)

    @pl.when(j == 0)
    def _():
        h = _norm_mod(x_ref[...], g_ref[...], mod_ref[0:1, :], mod_ref[1:2, :]).astype(BF16)
        h_ref[...] = h
        o_ref[...] = partial_out(h)

    @pl.when(jnp.logical_and(j > 0, j < last))
    def _():
        o_ref[...] += partial_out(h_ref[...])

    @pl.when(j == last)
    def _():
        acc = o_ref[...] + partial_out(h_ref[...])
        out = x_ref[...] + (FFN_RES * mod_ref[2:3, :]) * acc
        if final_norm:
            ms = jnp.mean(out * out, axis=-1, keepdims=True)
            out = (out * lax.rsqrt(ms + NORM_EPS)) * fg_ref[...]
        o_ref[...] = out


def _ffn(x, g, mod3, wg, wu, wd, layer, slot, final_g, *, final_norm, tm=1024, tf=256):
    t, d = x.shape
    f = wg.shape[3]
    return pl.pallas_call(
        functools.partial(_ffn_kernel, final_norm=final_norm),
        grid=(t // tm, f // tf),
        in_specs=[
            pl.BlockSpec((tm, d), lambda i, j: (i, 0)),
            pl.BlockSpec((1, d), lambda i, j: (0, 0)),
            pl.BlockSpec((3, d), lambda i, j: (0, 0)),
            pl.BlockSpec((None, None, d, tf), lambda i, j: (layer, slot, 0, j)),
            pl.BlockSpec((None, None, d, tf), lambda i, j: (layer, slot, 0, j)),
            pl.BlockSpec((None, None, tf, d), lambda i, j: (layer, slot, j, 0)),
            pl.BlockSpec((1, d), lambda i, j: (0, 0)),
        ],
        out_specs=pl.BlockSpec((tm, d), lambda i, j: (i, 0)),
        out_shape=jax.ShapeDtypeStruct((t, d), F32),
        scratch_shapes=[pltpu.VMEM((tm, d), BF16)],
        compiler_params=_params(("arbitrary", "arbitrary"), 60),
        name="ffn",
    )(x, g.reshape(1, d), mod3, wg, wu, wd, final_g.reshape(1, d))


def _nmm_kernel(x_ref, g_ref, mod_ref, w_ref, cs_ref, o_ref, h_ref, *, head_major):
    def project(h):
        r = jnp.dot(h, w_ref[...].astype(BF16), preferred_element_type=F32)
        r = (r * cs_ref[...]).astype(o_ref.dtype)
        if head_major:
            for hh in range(o_ref.shape[0]):
                o_ref[hh] = r[:, hh * LANES:(hh + 1) * LANES]
        else:
            o_ref[...] = r

    @pl.when(pl.program_id(1) == 0)
    def _():
        h = _norm_mod(x_ref[...], g_ref[...], mod_ref[0:1, :], mod_ref[1:2, :]).astype(BF16)
        h_ref[...] = h
        project(h)

    @pl.when(pl.program_id(1) > 0)
    def _():
        project(h_ref[...])


def _norm_mod_matmul(x, g, mod3, w, widx, colscale, *, head_major=False, tm=1024, tn=1024):
    t, d = x.shape
    n = w.shape[2]
    if head_major:
        out_spec = pl.BlockSpec((tn // LANES, tm, LANES), lambda i, j: (j, i, 0))
        out_shape = jax.ShapeDtypeStruct((n // LANES, t, LANES), BF16)
    else:
        out_spec = pl.BlockSpec((tm, tn), lambda i, j: (i, j))
        out_shape = jax.ShapeDtypeStruct((t, n), BF16)
    return pl.pallas_call(
        functools.partial(_nmm_kernel, head_major=head_major),
        grid=(t // tm, n // tn),
        in_specs=[
            pl.BlockSpec((tm, d), lambda i, j: (i, 0)),
            pl.BlockSpec((1, d), lambda i, j: (0, 0)),
            pl.BlockSpec((3, d), lambda i, j: (0, 0)),
            pl.BlockSpec((None, d, tn), lambda i, j: (widx, 0, j)),
            pl.BlockSpec((1, tn), lambda i, j: (0, j)),
        ],
        out_specs=out_spec,
        out_shape=out_shape,
        scratch_shapes=[pltpu.VMEM((tm, d), BF16)],
        compiler_params=_params(("arbitrary", "arbitrary"), 48),
        name="norm_mod_matmul",
    )(x, g.reshape(1, d), mod3, w, colscale.reshape(1, n))


def _proj_res_kernel(a_ref, w_ref, x_ref, gate_ref, o_ref):
    r = jnp.dot(a_ref[...], w_ref[...], preferred_element_type=F32)
    o_ref[...] = x_ref[...] + gate_ref[...] * r


def _proj_residual(a, w, widx, x, gate, *, tm=512):
    t, k = a.shape
    d = w.shape[2]
    return pl.pallas_call(
        _proj_res_kernel,
        grid=(t // tm,),
        in_specs=[
            pl.BlockSpec((tm, k), lambda i: (i, 0)),
            pl.BlockSpec((None, k, d), lambda i: (widx, 0, 0)),
            pl.BlockSpec((tm, d), lambda i: (i, 0)),
            pl.BlockSpec((1, d), lambda i: (0, 0)),
        ],
        out_specs=pl.BlockSpec((tm, d), lambda i: (i, 0)),
        out_shape=jax.ShapeDtypeStruct((t, d), F32),
        compiler_params=_params(("arbitrary",), 48),
        name="proj_residual",
    )(a, w, x, gate)


def _moba_key_const(t):
    pos = np.arange(t)
    lane = np.arange(LANES)[None, :]
    blk = (pos // MOBA_BLOCK)[:, None]
    inblk = (pos % MOBA_BLOCK)[:, None].astype(np.float32)
    kc = np.where(lane < 64, (blk == (lane % 32)).astype(np.float32), 0.0)
    kc = np.where((lane == 64) | (lane == 65), 1.0, kc)
    kc = np.where((lane == 66) | (lane == 67), inblk, kc)
    return jnp.asarray(kc, dtype=BF16)


def _alibi_slopes(n):
    return np.asarray([2.0 ** (-8.0 * (i + 1) / n) for i in range(n)], np.float32)


def _moba_kernel(slope_ref, q_ref, k_ref, v_ref, kc_ref, o_ref,
                 kaug, vaug, qaug, kmean_s, s_ring, mb_ring, acc_s, *, nblk):
    dh = MOBA_HEAD_DIM
    tq, tk = MOBA_QTILE, MOBA_KTILE
    own_tiles = tq // tk
    qt = pl.program_id(1)
    n_past = own_tiles * qt

    @pl.when(qt == 0)
    def _():
        k = k_ref[...]
        kaug[:, 0:dh] = k
        kaug[:, dh:2 * dh] = kc_ref[...]
        vaug[:, 0:dh] = v_ref[...]
        ones_col = lax.broadcasted_iota(jnp.int32, (vaug.shape[0], dh), 1) == 0
        vaug[:, dh:2 * dh] = jnp.where(ones_col, 1.0, 0.0).astype(BF16)
        kf = k.astype(F32).reshape(nblk, MOBA_BLOCK, dh)
        kmean_s[...] = jnp.zeros(kmean_s.shape, F32)
        kmean_s[0:nblk, :] = jnp.sum(kf, axis=1) * (1.0 / MOBA_BLOCK)

    q = q_ref[...]
    km_hi, km_lo = _split_bf16(kmean_s[...])
    gate_t = (lax.dot_general(km_hi.astype(BF16), q, NT_DIMS, preferred_element_type=F32)
              + lax.dot_general(km_lo.astype(BF16), q, NT_DIMS, preferred_element_type=F32))
    blk_t = lax.broadcasted_iota(jnp.int32, (MOBA_MAX_BLOCKS, tq), 0)
    row_t = lax.broadcasted_iota(jnp.int32, (MOBA_MAX_BLOCKS, tq), 1)
    qblk_t = (tq // MOBA_BLOCK) * qt + row_t // MOBA_BLOCK
    blk_tf = blk_t.astype(F32)
    neg_inf = -jnp.inf

    g = jnp.where(blk_t < qblk_t, gate_t[0:MOBA_MAX_BLOCKS, :], neg_inf)
    act_t = jnp.where(blk_t == qblk_t, 1.0, 0.0)
    for _ in range(MOBA_TOPK):
        mx = jnp.max(g, axis=0, keepdims=True)
        idx = jnp.min(jnp.where(g == mx, blk_tf, float(LANES)), axis=0, keepdims=True)
        hit = blk_tf == jnp.where(mx > neg_inf, idx, float(LANES))
        act_t = jnp.where(hit, 1.0, act_t)
        g = jnp.where(hit, neg_inf, g)
    pad = jnp.zeros((LANES - 2 * MOBA_MAX_BLOCKS, tq), F32)
    active = jnp.concatenate([act_t, act_t, pad], axis=0).T > 0.5

    lane = lax.broadcasted_iota(jnp.int32, (tq, LANES), 1)
    row = lax.broadcasted_iota(jnp.int32, (tq, LANES), 0)
    blk = lane & (MOBA_MAX_BLOCKS - 1)
    slope = slope_ref[...]
    offset = -slope * (tq * qt - MOBA_BLOCK * blk).astype(F32)
    off_hi, off_lo = _split_bf16(offset)
    row_hi, row_lo = _split_bf16(-slope * row.astype(F32))
    sl_hi, sl_lo = _split_bf16(jnp.broadcast_to(slope, (tq, LANES)))
    extra = jnp.where(lane < 32, jnp.where(active, off_hi, MASKED),
            jnp.where(lane < 64, jnp.where(active, off_lo, 0.0),
            jnp.where(lane == 64, row_hi,
            jnp.where(lane == 65, row_lo,
            jnp.where(lane == 66, sl_hi,
            jnp.where(lane == 67, sl_lo, 0.0))))))
    qaug[:, 0:dh] = q
    qaug[:, dh:2 * dh] = extra.astype(BF16)

    def key_tile(u):
        if own_tiles == 1:
            return jnp.where(u == 0, qt, jnp.maximum(jnp.minimum(u, qt) - 1, 0))
        past = jnp.maximum(jnp.minimum(u, n_past + own_tiles - 1) - own_tiles, 0)
        return jnp.where(u < own_tiles, n_past + u, past)

    def scores(u):
        tc = pl.multiple_of(key_tile(u) * tk, tk)
        return lax.dot_general(qaug[...], kaug[pl.ds(tc, tk), :], NT_DIMS,
                               preferred_element_type=F32)

    def own_scores(u):
        r2 = lax.broadcasted_iota(jnp.int32, (tq, tk), 0)
        c2 = lax.broadcasted_iota(jnp.int32, (tq, tk), 1)
        return jnp.where(c2 + u * tk <= r2, scores(u), MASKED)

    def running_max(s, prev):
        r = jnp.max(s, axis=1, keepdims=True)
        return jnp.maximum(prev, jnp.broadcast_to(r, (tq, LANES)))

    assert own_tiles in (1, 2)
    floor = jnp.full((tq, LANES), MAX_FLOOR, F32)
    s0 = own_scores(0)
    s_ring[0] = s0
    mb_ring[0] = running_max(s0, floor)
    mb_ring[1] = floor
    s_ring[1] = own_scores(1) if own_tiles == 2 else scores(1)
    acc_s[...] = jnp.zeros(acc_s.shape, F32)

    def step(c, carry):
        cur, nxt, nxt2 = lax.rem(c, 3), lax.rem(c + 1, 3), lax.rem(c + 2, 3)
        mcur, mprev = lax.rem(c, 2), lax.rem(c + 1, 2)
        mb = mb_ring[mcur]
        alpha = jnp.exp2(mb_ring[mprev] - mb)
        p = jnp.exp2(s_ring[cur] - jnp.concatenate([mb] * (tk // LANES), axis=1)).astype(BF16)
        tv = pl.multiple_of(key_tile(c) * tk, tk)
        pv = jnp.dot(p, vaug[pl.ds(tv, tk), :], preferred_element_type=F32)
        acc_s[...] = acc_s[...] * jnp.concatenate([alpha, alpha], axis=1) + pv
        mb_ring[mprev] = running_max(s_ring[nxt], mb)
        s_ring[nxt2] = scores(c + 2)
        return carry

    lax.fori_loop(0, n_past + own_tiles, step, 0)

    acc = acc_s[...]
    o_ref[...] = (acc[:, 0:dh] / acc[:, dh:dh + 1]).astype(o_ref.dtype)


def _moba_attention(qkv, t):
    h, dh, tq, tk = MOBA_HEADS, MOBA_HEAD_DIM, MOBA_QTILE, MOBA_KTILE
    nblk = t // MOBA_BLOCK
    assert t % tq == 0 and nblk <= MOBA_MAX_BLOCKS
    slopes = _alibi_slopes(h).astype(np.float64) * LOG2E
    slopes = jnp.asarray(np.broadcast_to(slopes.astype(np.float32)[:, None, None], (h, 1, LANES)))
    return pl.pallas_call(
        functools.partial(_moba_kernel, nblk=nblk),
        grid=(h, t // tq),
        in_specs=[
            pl.BlockSpec((None, 1, LANES), lambda hh, p: (hh, 0, 0)),
            pl.BlockSpec((None, tq, dh), lambda hh, p: (hh, p, 0)),
            pl.BlockSpec((None, t, dh), lambda hh, p: (h + hh, 0, 0)),
            pl.BlockSpec((None, t, dh), lambda hh, p: (2 * h + hh, 0, 0)),
            pl.BlockSpec((t, LANES), lambda hh, p: (0, 0)),
        ],
        out_specs=pl.BlockSpec((tq, dh), lambda hh, p: (p, hh)),
        out_shape=jax.ShapeDtypeStruct((t, h * dh), BF16),
        scratch_shapes=[
            pltpu.VMEM((t, 2 * dh), BF16),
            pltpu.VMEM((t, 2 * dh), BF16),
            pltpu.VMEM((tq, 2 * dh), BF16),
            pltpu.VMEM((LANES, dh), F32),
            pltpu.VMEM((3, tq, tk), F32),
            pltpu.VMEM((2, tq, LANES), F32),
            pltpu.VMEM((tq, 2 * dh), F32),
        ],
        compiler_params=_params(("arbitrary", "arbitrary"), 48),
        name="moba_attention",
    )(slopes, qkv, qkv, qkv, _moba_key_const(t))


def _pool_kernel(x_ref, xh_ref, g_ref, mod_ref, w_ref, ps_ref, o_ref, hbuf):
    i = pl.program_id(0)
    tm = x_ref.shape[0]
    grp = w_ref.shape[1]
    g, shift, scale, gate = g_ref[...], mod_ref[0:1, :], mod_ref[1:2, :], mod_ref[2:3, :]
    x = x_ref[...]
    h = _norm_mod(x, g, shift, scale)
    halo = _norm_mod(xh_ref[...], g, shift, scale)
    hbuf[0:POOL_HALO, :] = jnp.where(i == 0, 0.0, halo)
    hbuf[POOL_HALO:POOL_HALO + tm, :] = h
    tpos = i * tm + lax.broadcasted_iota(jnp.int32, (tm, 1), 0)
    for gi, win in enumerate(POOL_WINDOWS):
        cols = slice(gi * grp, (gi + 1) * grp)
        hg = h[:, cols]
        acc = hg
        for back in range(1, win):
            acc = acc + hbuf[POOL_HALO - back:POOL_HALO - back + tm, cols]
        cnt = jnp.minimum(tpos + 1, win).astype(F32)
        pooled = acc / cnt - hg
        mixed = jnp.dot(pooled.astype(BF16), w_ref[gi], preferred_element_type=F32)
        o_ref[:, cols] = x[:, cols] + gate[:, cols] * (mixed * ps_ref[:, cols])


def _pool_mixer(x, g, mod3, w_pool, pool_scale, *, tm=512):
    t, d = x.shape
    ng, grp, _ = w_pool.shape
    halo_blocks = tm // POOL_HALO
    return pl.pallas_call(
        _pool_kernel,
        grid=(t // tm,),
        in_specs=[
            pl.BlockSpec((tm, d), lambda i: (i, 0)),
            pl.BlockSpec((POOL_HALO, d), lambda i: (jnp.maximum(i * halo_blocks - 1, 0), 0)),
            pl.BlockSpec((1, d), lambda i: (0, 0)),
            pl.BlockSpec((3, d), lambda i: (0, 0)),
            pl.BlockSpec((ng, grp, grp), lambda i: (0, 0, 0)),
            pl.BlockSpec((1, d), lambda i: (0, 0)),
        ],
        out_specs=pl.BlockSpec((tm, d), lambda i: (i, 0)),
        out_shape=jax.ShapeDtypeStruct((t, d), F32),
        scratch_shapes=[pltpu.VMEM((POOL_HALO + tm, d), F32)],
        compiler_params=_params(("arbitrary",), 48),
        name="pool_mixer",
    )(x, x, g.reshape(1, d), mod3, w_pool, pool_scale.reshape(1, d))


def _swa_bias_const():
    w = SWA_WINDOW
    r = np.arange(w)[:, None]
    c = np.arange(2 * w)[None, :]
    dist = (r + w - c).astype(np.float32)
    valid = (dist >= 0) & (dist < w)
    slopes = _alibi_slopes(SWA_Q_HEADS)
    npair = SWA_GROUP // 2
    out = np.empty((SWA_KV_HEADS, npair * w, 2 * 2 * w), np.float32)
    for kv in range(SWA_KV_HEADS):
        for p in range(npair):
            for half in range(2):
                sl = slopes[kv * SWA_GROUP + 2 * p + half]
                tile = np.where(valid, -sl * dist, np.float32(MASKED)).astype(np.float32)
                out[kv, p * w:(p + 1) * w, half * 2 * w:(half + 1) * 2 * w] = tile
    return jnp.asarray(out)


def _swa_kernel(q_ref, kp_ref, kc_ref, vp_ref, vc_ref, bias_ref, sink_ref, o_ref):
    n = pl.program_id(0)
    w = SWA_WINDOW
    npair = SWA_GROUP // 2
    rows = npair * w
    lane = lax.broadcasted_iota(jnp.int32, (2 * w, LANES), 1)
    low = lane < SWA_HEAD_DIM
    col = lax.broadcasted_iota(jnp.int32, (rows, 4 * w), 1)
    prev_cols = (col & (2 * w - 1)) < w
    no_prev = jnp.logical_and(n == 0, prev_cols)
    out_low = lax.broadcasted_iota(jnp.int32, (rows, LANES), 1) < SWA_HEAD_DIM
    ones_even = jnp.where(low, 1.0, 0.0)
    ones_blk = jnp.concatenate([ones_even, 1.0 - ones_even], axis=0)
    for kv in range(SWA_KV_HEADS):
        ksl = slice(kv * LANES, (kv + 1) * LANES)
        kd = jnp.concatenate([kp_ref[:, ksl], kc_ref[:, ksl]], axis=0).astype(F32)
        vd = jnp.concatenate([vp_ref[:, ksl], vc_ref[:, ksl]], axis=0).astype(F32)
        kk = jnp.concatenate([jnp.where(low, kd, 0.0), jnp.where(low, 0.0, kd)], axis=0).astype(BF16)
        vv = jnp.concatenate([jnp.where(low, vd, 0.0), jnp.where(low, 0.0, vd)], axis=0)
        vv_aug = jnp.concatenate([vv, ones_blk], axis=1).astype(BF16)
        base = kv * npair * LANES
        qs = jnp.concatenate(
            [q_ref[:, base + p * LANES:base + (p + 1) * LANES] for p in range(npair)], axis=0)
        s = lax.dot_general(qs, kk, NT_DIMS, preferred_element_type=F32) + bias_ref[kv]
        s = jnp.where(no_prev, MASKED, s)
        es, ms = [], []
        for half in range(2):
            sh = s[:, half * 2 * w:(half + 1) * 2 * w]
            r = jnp.max(sh, axis=1, keepdims=True)
            m = jnp.maximum(jnp.broadcast_to(r, (rows, LANES)), sink_ref[kv, half])
            es.append(jnp.exp(sh - jnp.concatenate([m, m], axis=1)))
            ms.append(m)
        p = jnp.concatenate(es, axis=1).astype(BF16)
        o = jnp.dot(p, vv_aug, preferred_element_type=F32)
        m_mix = jnp.where(out_low, ms[0], ms[1])
        sink_mix = jnp.where(out_low, sink_ref[kv, 0], sink_ref[kv, 1])
        o = o[:, 0:LANES] / (o[:, LANES:2 * LANES] + jnp.exp(sink_mix - m_mix))
        for pp in range(npair):
            o_ref[:, base + pp * LANES:base + (pp + 1) * LANES] = (
                o[pp * w:(pp + 1) * w, :].astype(o_ref.dtype))


def _swa_attention(q, kvdup, sinks, t):
    w = SWA_WINDOW
    dq = SWA_Q_HEADS * SWA_HEAD_DIM
    dkv = SWA_KV_HEADS * LANES
    npair = SWA_GROUP // 2
    sink_tab = jnp.broadcast_to(
        sinks.astype(F32).reshape(SWA_KV_HEADS, npair, 2).transpose(0, 2, 1)[:, :, :, None, None],
        (SWA_KV_HEADS, 2, npair, w, LANES)).reshape(SWA_KV_HEADS, 2, npair * w, LANES)
    prev = lambda n: jnp.maximum(n - 1, 0)
    return pl.pallas_call(
        _swa_kernel,
        grid=(t // w,),
        in_specs=[
            pl.BlockSpec((w, dq), lambda n: (n, 0)),
            pl.BlockSpec((w, dkv), lambda n: (prev(n), 0)),
            pl.BlockSpec((w, dkv), lambda n: (n, 0)),
            pl.BlockSpec((w, dkv), lambda n: (prev(n), 1)),
            pl.BlockSpec((w, dkv), lambda n: (n, 1)),
            pl.BlockSpec((SWA_KV_HEADS, npair * w, 4 * w), lambda n: (0, 0, 0)),
            pl.BlockSpec((SWA_KV_HEADS, 2, npair * w, LANES), lambda n: (0, 0, 0, 0)),
        ],
        out_specs=pl.BlockSpec((w, dq), lambda n: (n, 0)),
        out_shape=jax.ShapeDtypeStruct((t, dq), BF16),
        compiler_params=_params(("arbitrary",), 48),
        name="swa_attention",
    )(q, kvdup, kvdup, kvdup, kvdup, _swa_bias_const(), sink_tab)


def kernel(x, c, norm_g, ada_w, ada_b, ffn_w_gate, ffn_w_up, ffn_w_down, moba_w_qkv, moba_w_o,
           pool_w, pool_scale, swa_w_qkv, swa_w_o, swa_sinks, final_g):
    b, t, d = x.shape
    assert b == 1
    depth = norm_g.shape[0]
    mods = _ada_mod(c, ada_w, ada_b)

    wg, wu, wd = ffn_w_gate, ffn_w_up, ffn_w_down
    moba_o_bf16 = moba_w_o.astype(BF16)
    swa_o_bf16 = swa_w_o.astype(BF16)

    def ffn(xs, i, s, mod3, final_norm):
        return _ffn(xs, norm_g[i, 2 * s], mod3, wg, wu, wd, i, s, final_g, final_norm=final_norm)

    xs = x.reshape(t, d)
    for i in range(depth):
        mod = mods[i]
        last = i == depth - 1
        xs = ffn(xs, i, 0, mod[0:3], False)
        kind, j = i % 3, i // 3
        m2 = mod[3:6]
        gate2 = mod[5:6]
        if kind == 0:
            qscale = jnp.concatenate([jnp.full((d,), MOBA_HEAD_DIM ** -0.5 * LOG2E, F32), jnp.ones((2 * d,), F32)])
            qkv = _norm_mod_matmul(xs, norm_g[i, 1], m2, moba_w_qkv, j, qscale, head_major=True)
            o = _moba_attention(qkv, t)
            xs = _proj_residual(o, moba_o_bf16, j, xs, gate2)
        elif kind == 1:
            xs = _pool_mixer(xs, norm_g[i, 1], m2, pool_w[j].astype(BF16), pool_scale[j])
        else:
            dq = SWA_Q_HEADS * SWA_HEAD_DIM
            hd = SWA_HEAD_DIM
            wqkv = swa_w_qkv[j]
            wq = wqkv[:, :dq]
            wk = wqkv[:, dq:dq + SWA_KV_HEADS * hd].reshape(d, SWA_KV_HEADS, 1, hd)
            wv = wqkv[:, dq + SWA_KV_HEADS * hd:].reshape(d, SWA_KV_HEADS, 1, hd)
            dup = lambda a: jnp.broadcast_to(a, (d, SWA_KV_HEADS, 2, hd)).reshape(d, SWA_KV_HEADS * 2 * hd)
            wkv = jnp.concatenate([dup(wk), dup(wv)], axis=1)
            q = _norm_mod_matmul(xs, norm_g[i, 1], m2, wq[None], 0,
                                 jnp.full((dq,), hd ** -0.5, F32))
            kvdup = _norm_mod_matmul(xs, norm_g[i, 1], m2, wkv[None], 0, jnp.ones((wkv.shape[1],), F32))
            o = _swa_attention(q, kvdup, swa_sinks[j], t)
            xs = _proj_residual(o, swa_o_bf16, j, xs, gate2)
        xs = ffn(xs, i, 1, mod[6:9], last)
    return xs.reshape(b, t, d)
```

```python
import functools

import numpy as np
import jax
import jax.numpy as jnp
from jax import lax
from jax.experimental import pallas as pl
from jax.experimental.pallas import tpu as pltpu

F32 = jnp.float32
BF16 = jnp.bfloat16

NORM_EPS = 1e-6
FFN_RES = 0.5
MASKED = -1e30
MAX_FLOOR = -1e20
LOG2E = 1.4426950408889634

LANES = 128
V7X_VMEM_BYTES = 64 * 1024 * 1024

MOBA_HEADS = 16
MOBA_HEAD_DIM = 128
MOBA_BLOCK = 256
MOBA_TOPK = 3
MOBA_QTILE = 2 * MOBA_BLOCK
MOBA_KTILE = 2 * MOBA_BLOCK
MOBA_MAX_BLOCKS = 32

POOL_WINDOWS = (2, 4, 8, 16)
POOL_HALO = 16

SWA_HEAD_DIM = 64
SWA_Q_HEADS = 32
SWA_KV_HEADS = 4
SWA_GROUP = SWA_Q_HEADS // SWA_KV_HEADS
SWA_WINDOW = 128

NT_DIMS = (((1,), (1,)), ((), ()))


def _params(semantics, vmem_mib):
    return pltpu.CompilerParams(dimension_semantics=semantics,
                                vmem_limit_bytes=vmem_mib * 1024 * 1024)


def _norm_mod(xf, g, shift, scale):
    ms = jnp.mean(xf * xf, axis=-1, keepdims=True)
    y = xf * lax.rsqrt(ms + NORM_EPS)
    return y * (g * (1.0 + scale)) + shift


def _split_bf16(v):
    hi = v.astype(BF16).astype(F32)
    lo = (v - hi).astype(BF16).astype(F32)
    return hi, lo


def _ada_kernel(c_ref, w_ref, b_ref, o_ref):
    cb = c_ref[...]
    cs = cb * jax.nn.sigmoid(cb)
    w = w_ref[...]
    prod = w * jnp.concatenate([cs] * (w.shape[1] // LANES), axis=1)
    o_ref[...] = jnp.sum(prod, axis=0, keepdims=True) + b_ref[...]


def _ada_mod(c, ada_w, ada_b, tn=1024):
    depth, d, n = ada_w.shape
    c_rep = jnp.broadcast_to(c.reshape(d, 1), (d, LANES))
    out = pl.pallas_call(
        _ada_kernel,
        grid=(depth, n // tn),
        in_specs=[
            pl.BlockSpec((d, LANES), lambda l, j: (0, 0)),
            pl.BlockSpec((None, d, tn), lambda l, j: (l, 0, j)),
            pl.BlockSpec((None, 1, tn), lambda l, j: (l, 0, j)),
        ],
        out_specs=pl.BlockSpec((None, 1, tn), lambda l, j: (l, 0, j)),
        out_shape=jax.ShapeDtypeStruct((depth, 1, n), F32),
        compiler_params=_params(("arbitrary", "arbitrary"), 40),
        name="ada_mod",
    )(c_rep, ada_w, ada_b.reshape(depth, 1, n))
    return out.reshape(depth, n // d, d)


def _ffn_kernel(x_ref, g_ref, mod_ref, wg_ref, wu_ref, wd_ref, fg_ref, o_ref, h_ref, *, final_norm):
    j = pl.program_id(1)
    last = pl.num_programs(1) - 1

    def partial_out(h):
        gg = jnp.dot(h, wg_ref[...].astype(BF16), preferred_element_type=F32)
        uu = jnp.dot(h, wu_ref[...].astype(BF16), preferred_element_type=F32)
        a = ((gg * jax.nn.sigmoid(gg)) * uu).astype(BF16)
        return jnp.dot(a, wd_ref[...].astype(BF16), preferred_element_type=F32)

    @pl.when(j == 0)
    def _():
        h = _norm_mod(x_ref[...], g_ref[...], mod_ref[0:1, :], mod_ref[1:2, :]).astype(BF16)
        h_ref[...] = h
        o_ref[...] = partial_out(h)

    @pl.when(jnp.logical_and(j > 0, j < last))
    def _():
        o_ref[...] += partial_out(h_ref[...])

    @pl.when(j == last)
    def _():
        acc = o_ref[...] + partial_out(h_ref[...])
        out = x_ref[...] + (FFN_RES * mod_ref[2:3, :]) * acc
        if final_norm:
            ms = jnp.mean(out * out, axis=-1, keepdims=True)
            out = (out * lax.rsqrt(ms + NORM_EPS)) * fg_ref[...]
        o_ref[...] = out


def _ffn(x, g, mod3, wg, wu, wd, layer, slot, final_g, *, final_norm, tm=1024, tf=256):
    t, d = x.shape
    f = wg.shape[3]
    return pl.pallas_call(
        functools.partial(_ffn_kernel, final_norm=final_norm),
        grid=(t // tm, f // tf),
        in_specs=[
            pl.BlockSpec((tm, d), lambda i, j: (i, 0)),
            pl.BlockSpec((1, d), lambda i, j: (0, 0)),
            pl.BlockSpec((3, d), lambda i, j: (0, 0)),
            pl.BlockSpec((None, None, d, tf), lambda i, j: (layer, slot, 0, j)),
            pl.BlockSpec((None, None, d, tf), lambda i, j: (layer, slot, 0, j)),
            pl.BlockSpec((None, None, tf, d), lambda i, j: (layer, slot, j, 0)),
            pl.BlockSpec((1, d), lambda i, j: (0, 0)),
        ],
        out_specs=pl.BlockSpec((tm, d), lambda i, j: (i, 0)),
        out_shape=jax.ShapeDtypeStruct((t, d), F32),
        scratch_shapes=[pltpu.VMEM((tm, d), BF16)],
        compiler_params=_params(("arbitrary", "arbitrary"), 60),
        name="ffn",
    )(x, g.reshape(1, d), mod3, wg, wu, wd, final_g.reshape(1, d))


def _nmm_kernel(x_ref, g_ref, mod_ref, w_ref, cs_ref, o_ref, h_ref, *, head_major):
    def project(h):
        r = jnp.dot(h, w_ref[...].astype(BF16), preferred_element_type=F32)
        r = (r * cs_ref[...]).astype(o_ref.dtype)
        if head_major:
            for hh in range(o_ref.shape[0]):
                o_ref[hh] = r[:, hh * LANES:(hh + 1) * LANES]
        else:
            o_ref[...] = r

    @pl.when(pl.program_id(1) == 0)
    def _():
        h = _norm_mod(x_ref[...], g_ref[...], mod_ref[0:1, :], mod_ref[1:2, :]).astype(BF16)
        h_ref[...] = h
        project(h)

    @pl.when(pl.program_id(1) > 0)
    def _():
        project(h_ref[...])


def _norm_mod_matmul(x, g, mod3, w, widx, colscale, *, head_major=False, tm=1024, tn=1024):
    t, d = x.shape
    n = w.shape[2]
    if head_major:
        out_spec = pl.BlockSpec((tn // LANES, tm, LANES), lambda i, j: (j, i, 0))
        out_shape = jax.ShapeDtypeStruct((n // LANES, t, LANES), BF16)
    else:
        out_spec = pl.BlockSpec((tm, tn), lambda i, j: (i, j))
        out_shape = jax.ShapeDtypeStruct((t, n), BF16)
    return pl.pallas_call(
        functools.partial(_nmm_kernel, head_major=head_major),
        grid=(t // tm, n // tn),
        in_specs=[
            pl.BlockSpec((tm, d), lambda i, j: (i, 0)),
            pl.BlockSpec((1, d), lambda i, j: (0, 0)),
            pl.BlockSpec((3, d), lambda i, j: (0, 0)),
            pl.BlockSpec((None, d, tn), lambda i, j: (widx, 0, j)),
            pl.BlockSpec((1, tn), lambda i, j: (0, j)),
        ],
        out_specs=out_spec,
        out_shape=out_shape,
        scratch_shapes=[pltpu.VMEM((tm, d), BF16)],
        compiler_params=_params(("arbitrary", "arbitrary"), 48),
        name="norm_mod_matmul",
    )(x, g.reshape(1, d), mod3, w, colscale.reshape(1, n))


def _proj_res_kernel(a_ref, w_ref, x_ref, gate_ref, o_ref):
    r = jnp.dot(a_ref[...], w_ref[...], preferred_element_type=F32)
    o_ref[...] = x_ref[...] + gate_ref[...] * r


def _proj_residual(a, w, widx, x, gate, *, tm=512):
    t, k = a.shape
    d = w.shape[2]
    return pl.pallas_call(
        _proj_res_kernel,
        grid=(t // tm,),
        in_specs=[
            pl.BlockSpec((tm, k), lambda i: (i, 0)),
            pl.BlockSpec((None, k, d), lambda i: (widx, 0, 0)),
            pl.BlockSpec((tm, d), lambda i: (i, 0)),
            pl.BlockSpec((1, d), lambda i: (0, 0)),
        ],
        out_specs=pl.BlockSpec((tm, d), lambda i: (i, 0)),
        out_shape=jax.ShapeDtypeStruct((t, d), F32),
        compiler_params=_params(("arbitrary",), 48),
        name="proj_residual",
    )(a, w, x, gate)


def _moba_key_const(t):
    pos = np.arange(t)
    lane = np.arange(LANES)[None, :]
    blk = (pos // MOBA_BLOCK)[:, None]
    inblk = (pos % MOBA_BLOCK)[:, None].astype(np.float32)
    kc = np.where(lane < 64, (blk == (lane % 32)).astype(np.float32), 0.0)
    kc = np.where((lane == 64) | (lane == 65), 1.0, kc)
    kc = np.where((lane == 66) | (lane == 67), inblk, kc)
    return jnp.asarray(kc.T, dtype=BF16)


def _alibi_slopes(n):
    return np.asarray([2.0 ** (-8.0 * (i + 1) / n) for i in range(n)], np.float32)


def _moba_kernel(slope_ref, q_ref, k_ref, v_ref, kc_ref, o_ref,
                 kaug_t, vaug, qaug, kmean_s, s_ring, mb_ring, acc_s, *, nblk):
    dh = MOBA_HEAD_DIM
    tq, tk = MOBA_QTILE, MOBA_KTILE
    own_tiles = tq // tk
    qt = pl.program_id(1)
    n_past = own_tiles * qt

    @pl.when(qt == 0)
    def _():
        k = k_ref[...]
        for j in range(kaug_t.shape[0]):
            kaug_t[j, 0:dh, :] = k[j * tk:(j + 1) * tk, :].astype(F32).T.astype(BF16)
            kaug_t[j, dh:2 * dh, :] = kc_ref[:, j * tk:(j + 1) * tk]
        vaug[:, 0:dh] = v_ref[...]
        ones_col = lax.broadcasted_iota(jnp.int32, (vaug.shape[0], dh), 1) == 0
        vaug[:, dh:2 * dh] = jnp.where(ones_col, 1.0, 0.0).astype(BF16)
        kf = k.astype(F32).reshape(nblk, MOBA_BLOCK, dh)
        kmean_s[...] = jnp.zeros(kmean_s.shape, F32)
        kmean_s[0:nblk, :] = jnp.sum(kf, axis=1) * (1.0 / MOBA_BLOCK)

    q = q_ref[...]
    km_hi, km_lo = _split_bf16(kmean_s[...])
    gate_t = (lax.dot_general(km_hi.astype(BF16), q, NT_DIMS, preferred_element_type=F32)
              + lax.dot_general(km_lo.astype(BF16), q, NT_DIMS, preferred_element_type=F32))
    blk_t = lax.broadcasted_iota(jnp.int32, (MOBA_MAX_BLOCKS, tq), 0)
    row_t = lax.broadcasted_iota(jnp.int32, (MOBA_MAX_BLOCKS, tq), 1)
    qblk_t = (tq // MOBA_BLOCK) * qt + row_t // MOBA_BLOCK
    blk_tf = blk_t.astype(F32)
    neg_inf = -jnp.inf

    g = jnp.where(blk_t < qblk_t, gate_t[0:MOBA_MAX_BLOCKS, :], neg_inf)
    act_t = jnp.where(blk_t == qblk_t, 1.0, 0.0)
    for _ in range(MOBA_TOPK):
        mx = jnp.max(g, axis=0, keepdims=True)
        idx = jnp.min(jnp.where(g == mx, blk_tf, float(LANES)), axis=0, keepdims=True)
        hit = blk_tf == jnp.where(mx > neg_inf, idx, float(LANES))
        act_t = jnp.where(hit, 1.0, act_t)
        g = jnp.where(hit, neg_inf, g)
    pad = jnp.zeros((LANES - 2 * MOBA_MAX_BLOCKS, tq), F32)
    active = jnp.concatenate([act_t, act_t, pad], axis=0).T > 0.5

    lane = lax.broadcasted_iota(jnp.int32, (tq, LANES), 1)
    row = lax.broadcasted_iota(jnp.int32, (tq, LANES), 0)
    blk = lane & (MOBA_MAX_BLOCKS - 1)
    slope = slope_ref[...]
    offset = -slope * (tq * qt - MOBA_BLOCK * blk).astype(F32)
    off_hi, off_lo = _split_bf16(offset)
    row_hi, row_lo = _split_bf16(-slope * row.astype(F32))
    sl_hi, sl_lo = _split_bf16(jnp.broadcast_to(slope, (tq, LANES)))
    extra = jnp.where(lane < 32, jnp.where(active, off_hi, MASKED),
            jnp.where(lane < 64, jnp.where(active, off_lo, 0.0),
            jnp.where(lane == 64, row_hi,
            jnp.where(lane == 65, row_lo,
            jnp.where(lane == 66, sl_hi,
            jnp.where(lane == 67, sl_lo, 0.0))))))
    qaug[:, 0:dh] = q
    qaug[:, dh:2 * dh] = extra.astype(BF16)

    def key_tile(u):
        if own_tiles == 1:
            return jnp.where(u == 0, qt, jnp.maximum(jnp.minimum(u, qt) - 1, 0))
        past = jnp.maximum(jnp.minimum(u, n_past + own_tiles - 1) - own_tiles, 0)
        return jnp.where(u < own_tiles, n_past + u, past)

    def scores(u):
        return jnp.dot(qaug[...], kaug_t[key_tile(u)], preferred_element_type=F32)

    def own_scores(u):
        r2 = lax.broadcasted_iota(jnp.int32, (tq, tk), 0)
        c2 = lax.broadcasted_iota(jnp.int32, (tq, tk), 1)
        return jnp.where(c2 + u * tk <= r2, scores(u), MASKED)

    def running_max(s, prev):
        r = jnp.max(s, axis=1, keepdims=True)
        return jnp.maximum(prev, jnp.broadcast_to(r, (tq, LANES)))

    assert own_tiles in (1, 2)
    floor = jnp.full((tq, LANES), MAX_FLOOR, F32)
    s0 = own_scores(0)
    s_ring[0] = s0
    mb_ring[0] = running_max(s0, floor)
    mb_ring[1] = floor
    s_ring[1] = own_scores(1) if own_tiles == 2 else scores(1)
    acc_s[...] = jnp.zeros(acc_s.shape, F32)

    def step(c, carry):
        cur, nxt, nxt2 = lax.rem(c, 3), lax.rem(c + 1, 3), lax.rem(c + 2, 3)
        mcur, mprev = lax.rem(c, 2), lax.rem(c + 1, 2)
        mb = mb_ring[mcur]
        alpha = jnp.exp2(mb_ring[mprev] - mb)
        p = jnp.exp2(s_ring[cur] - jnp.concatenate([mb] * (tk // LANES), axis=1)).astype(BF16)
        tv = pl.multiple_of(key_tile(c) * tk, tk)
        pv = jnp.dot(p, vaug[pl.ds(tv, tk), :], preferred_element_type=F32)
        acc_s[...] = acc_s[...] * jnp.concatenate([alpha, alpha], axis=1) + pv
        mb_ring[mprev] = running_max(s_ring[nxt], mb)
        s_ring[nxt2] = scores(c + 2)
        return carry

    lax.fori_loop(0, n_past + own_tiles, step, 0)

    acc = acc_s[...]
    o_ref[...] = (acc[:, 0:dh] / acc[:, dh:dh + 1]).astype(o_ref.dtype)


def _moba_attention(qkv, t):
    h, dh, tq, tk = MOBA_HEADS, MOBA_HEAD_DIM, MOBA_QTILE, MOBA_KTILE
    nblk = t // MOBA_BLOCK
    assert t % tq == 0 and nblk <= MOBA_MAX_BLOCKS
    slopes = _alibi_slopes(h).astype(np.float64) * LOG2E
    slopes = jnp.asarray(np.broadcast_to(slopes.astype(np.float32)[:, None, None], (h, 1, LANES)))
    return pl.pallas_call(
        functools.partial(_moba_kernel, nblk=nblk),
        grid=(h, t // tq),
        in_specs=[
            pl.BlockSpec((None, 1, LANES), lambda hh, p: (hh, 0, 0)),
            pl.BlockSpec((None, tq, dh), lambda hh, p: (hh, p, 0)),
            pl.BlockSpec((None, t, dh), lambda hh, p: (h + hh, 0, 0)),
            pl.BlockSpec((None, t, dh), lambda hh, p: (2 * h + hh, 0, 0)),
            pl.BlockSpec((LANES, t), lambda hh, p: (0, 0)),
        ],
        out_specs=pl.BlockSpec((tq, dh), lambda hh, p: (p, hh)),
        out_shape=jax.ShapeDtypeStruct((t, h * dh), BF16),
        scratch_shapes=[
            pltpu.VMEM((t // tk, 2 * dh, tk), BF16),
            pltpu.VMEM((t, 2 * dh), BF16),
            pltpu.VMEM((tq, 2 * dh), BF16),
            pltpu.VMEM((LANES, dh), F32),
            pltpu.VMEM((3, tq, tk), F32),
            pltpu.VMEM((2, tq, LANES), F32),
            pltpu.VMEM((tq, 2 * dh), F32),
        ],
        compiler_params=_params(("arbitrary", "arbitrary"), 48),
        name="moba_attention",
    )(slopes, qkv, qkv, qkv, _moba_key_const(t))


def _pool_kernel(x_ref, xh_ref, g_ref, mod_ref, w_ref, ps_ref, o_ref, hbuf):
    i = pl.program_id(0)
    tm = x_ref.shape[0]
    grp = w_ref.shape[1]
    g, shift, scale, gate = g_ref[...], mod_ref[0:1, :], mod_ref[1:2, :], mod_ref[2:3, :]
    x = x_ref[...]
    h = _norm_mod(x, g, shift, scale)
    halo = _norm_mod(xh_ref[...], g, shift, scale)
    hbuf[0:POOL_HALO, :] = jnp.where(i == 0, 0.0, halo)
    hbuf[POOL_HALO:POOL_HALO + tm, :] = h
    tpos = i * tm + lax.broadcasted_iota(jnp.int32, (tm, 1), 0)
    for gi, win in enumerate(POOL_WINDOWS):
        cols = slice(gi * grp, (gi + 1) * grp)
        hg = h[:, cols]
        acc = hg
        for back in range(1, win):
            acc = acc + hbuf[POOL_HALO - back:POOL_HALO - back + tm, cols]
        cnt = jnp.minimum(tpos + 1, win).astype(F32)
        pooled = acc / cnt - hg
        mixed = jnp.dot(pooled.astype(BF16), w_ref[gi], preferred_element_type=F32)
        o_ref[:, cols] = x[:, cols] + gate[:, cols] * (mixed * ps_ref[:, cols])


def _pool_mixer(x, g, mod3, w_pool, pool_scale, *, tm=512):
    t, d = x.shape
    ng, grp, _ = w_pool.shape
    halo_blocks = tm // POOL_HALO
    return pl.pallas_call(
        _pool_kernel,
        grid=(t // tm,),
        in_specs=[
            pl.BlockSpec((tm, d), lambda i: (i, 0)),
            pl.BlockSpec((POOL_HALO, d), lambda i: (jnp.maximum(i * halo_blocks - 1, 0), 0)),
            pl.BlockSpec((1, d), lambda i: (0, 0)),
            pl.BlockSpec((3, d), lambda i: (0, 0)),
            pl.BlockSpec((ng, grp, grp), lambda i: (0, 0, 0)),
            pl.BlockSpec((1, d), lambda i: (0, 0)),
        ],
        out_specs=pl.BlockSpec((tm, d), lambda i: (i, 0)),
        out_shape=jax.ShapeDtypeStruct((t, d), F32),
        scratch_shapes=[pltpu.VMEM((POOL_HALO + tm, d), F32)],
        compiler_params=_params(("arbitrary",), 48),
        name="pool_mixer",
    )(x, x, g.reshape(1, d), mod3, w_pool, pool_scale.reshape(1, d))


def _swa_bias_const():
    w = SWA_WINDOW
    r = np.arange(w)[:, None]
    c = np.arange(2 * w)[None, :]
    dist = (r + w - c).astype(np.float32)
    valid = (dist >= 0) & (dist < w)
    slopes = _alibi_slopes(SWA_Q_HEADS)
    npair = SWA_GROUP // 2
    out = np.empty((SWA_KV_HEADS, npair * w, 2 * 2 * w), np.float32)
    for kv in range(SWA_KV_HEADS):
        for p in range(npair):
            for half in range(2):
                sl = slopes[kv * SWA_GROUP + 2 * p + half]
                tile = np.where(valid, -sl * dist, np.float32(MASKED)).astype(np.float32)
                out[kv, p * w:(p + 1) * w, half * 2 * w:(half + 1) * 2 * w] = tile
    return jnp.asarray(out)


def _swa_kernel(q_ref, kp_ref, kc_ref, vp_ref, vc_ref, bias_ref, sink_ref, o_ref):
    n = pl.program_id(0)
    w = SWA_WINDOW
    npair = SWA_GROUP // 2
    rows = npair * w
    lane = lax.broadcasted_iota(jnp.int32, (2 * w, LANES), 1)
    low = lane < SWA_HEAD_DIM
    col = lax.broadcasted_iota(jnp.int32, (rows, 4 * w), 1)
    prev_cols = (col & (2 * w - 1)) < w
    no_prev = jnp.logical_and(n == 0, prev_cols)
    out_low = lax.broadcasted_iota(jnp.int32, (rows, LANES), 1) < SWA_HEAD_DIM
    ones_even = jnp.where(low, 1.0, 0.0)
    ones_blk = jnp.concatenate([ones_even, 1.0 - ones_even], axis=0)
    for kv in range(SWA_KV_HEADS):
        ksl = slice(kv * LANES, (kv + 1) * LANES)
        kd = jnp.concatenate([kp_ref[:, ksl], kc_ref[:, ksl]], axis=0).astype(F32)
        vd = jnp.concatenate([vp_ref[:, ksl], vc_ref[:, ksl]], axis=0).astype(F32)
        kk = jnp.concatenate([jnp.where(low, kd, 0.0), jnp.where(low, 0.0, kd)], axis=0).astype(BF16)
        vv = jnp.concatenate([jnp.where(low, vd, 0.0), jnp.where(low, 0.0, vd)], axis=0)
        vv_aug = jnp.concatenate([vv, ones_blk], axis=1).astype(BF16)
        base = kv * npair * LANES
        qs = jnp.concatenate(
            [q_ref[:, base + p * LANES:base + (p + 1) * LANES] for p in range(npair)], axis=0)
        s = lax.dot_general(qs, kk, NT_DIMS, preferred_element_type=F32) + bias_ref[kv]
        s = jnp.where(no_prev, MASKED, s)
        es, ms = [], []
        for half in range(2):
            sh = s[:, half * 2 * w:(half + 1) * 2 * w]
            r = jnp.max(sh, axis=1, keepdims=True)
            m = jnp.maximum(jnp.broadcast_to(r, (rows, LANES)), sink_ref[kv, half])
            es.append(jnp.exp(sh - jnp.concatenate([m, m], axis=1)))
            ms.append(m)
        p = jnp.concatenate(es, axis=1).astype(BF16)
        o = jnp.dot(p, vv_aug, preferred_element_type=F32)
        m_mix = jnp.where(out_low, ms[0], ms[1])
        sink_mix = jnp.where(out_low, sink_ref[kv, 0], sink_ref[kv, 1])
        o = o[:, 0:LANES] / (o[:, LANES:2 * LANES] + jnp.exp(sink_mix - m_mix))
        for pp in range(npair):
            o_ref[:, base + pp * LANES:base + (pp + 1) * LANES] = (
                o[pp * w:(pp + 1) * w, :].astype(o_ref.dtype))


def _swa_attention(q, kvdup, sinks, t):
    w = SWA_WINDOW
    dq = SWA_Q_HEADS * SWA_HEAD_DIM
    dkv = SWA_KV_HEADS * LANES
    npair = SWA_GROUP // 2
    sink_tab = jnp.broadcast_to(
        sinks.astype(F32).reshape(SWA_KV_HEADS, npair, 2).transpose(0, 2, 1)[:, :, :, None, None],
        (SWA_KV_HEADS, 2, npair, w, LANES)).reshape(SWA_KV_HEADS, 2, npair * w, LANES)
    prev = lambda n: jnp.maximum(n - 1, 0)
    return pl.pallas_call(
        _swa_kernel,
        grid=(t // w,),
        in_specs=[
            pl.BlockSpec((w, dq), lambda n: (n, 0)),
            pl.BlockSpec((w, dkv), lambda n: (prev(n), 0)),
            pl.BlockSpec((w, dkv), lambda n: (n, 0)),
            pl.BlockSpec((w, dkv), lambda n: (prev(n), 1)),
            pl.BlockSpec((w, dkv), lambda n: (n, 1)),
            pl.BlockSpec((SWA_KV_HEADS, npair * w, 4 * w), lambda n: (0, 0, 0)),
            pl.BlockSpec((SWA_KV_HEADS, 2, npair * w, LANES), lambda n: (0, 0, 0, 0)),
        ],
        out_specs=pl.BlockSpec((w, dq), lambda n: (n, 0)),
        out_shape=jax.ShapeDtypeStruct((t, dq), BF16),
        compiler_params=_params(("arbitrary",), 48),
        name="swa_attention",
    )(q, kvdup, kvdup, kvdup, kvdup, _swa_bias_const(), sink_tab)


def kernel(x, c, norm_g, ada_w, ada_b, ffn_w_gate, ffn_w_up, ffn_w_down, moba_w_qkv, moba_w_o,
           pool_w, pool_scale, swa_w_qkv, swa_w_o, swa_sinks, final_g):
    b, t, d = x.shape
    assert b == 1
    depth = norm_g.shape[0]
    mods = _ada_mod(c, ada_w, ada_b)

    wg, wu, wd = ffn_w_gate, ffn_w_up, ffn_w_down
    moba_o_bf16 = moba_w_o.astype(BF16)
    swa_o_bf16 = swa_w_o.astype(BF16)

    def ffn(xs, i, s, mod3, final_norm):
        return _ffn(xs, norm_g[i, 2 * s], mod3, wg, wu, wd, i, s, final_g, final_norm=final_norm)

    xs = x.reshape(t, d)
    for i in range(depth):
        mod = mods[i]
        last = i == depth - 1
        xs = ffn(xs, i, 0, mod[0:3], False)
        kind, j = i % 3, i // 3
        m2 = mod[3:6]
        gate2 = mod[5:6]
        if kind == 0:
            qscale = jnp.concatenate([jnp.full((d,), MOBA_HEAD_DIM ** -0.5 * LOG2E, F32), jnp.ones((2 * d,), F32)])
            qkv = _norm_mod_matmul(xs, norm_g[i, 1], m2, moba_w_qkv, j, qscale, head_major=True)
            o = _moba_attention(qkv, t)
            xs = _proj_residual(o, moba_o_bf16, j, xs, gate2)
        elif kind == 1:
            xs = _pool_mixer(xs, norm_g[i, 1], m2, pool_w[j].astype(BF16), pool_scale[j])
        else:
            dq = SWA_Q_HEADS * SWA_HEAD_DIM
            hd = SWA_HEAD_DIM
            wqkv = swa_w_qkv[j]
            wq = wqkv[:, :dq]
            wk = wqkv[:, dq:dq + SWA_KV_HEADS * hd].reshape(d, SWA_KV_HEADS, 1, hd)
            wv = wqkv[:, dq + SWA_KV_HEADS * hd:].reshape(d, SWA_KV_HEADS, 1, hd)
            dup = lambda a: jnp.broadcast_to(a, (d, SWA_KV_HEADS, 2, hd)).reshape(d, SWA_KV_HEADS * 2 * hd)
            wkv = jnp.concatenate([dup(wk), dup(wv)], axis=1)
            q = _norm_mod_matmul(xs, norm_g[i, 1], m2, wq[None], 0,
                                 jnp.full((dq,), hd ** -0.5, F32))
            kvdup = _norm_mod_matmul(xs, norm_g[i, 1], m2, wkv[None], 0, jnp.ones((wkv.shape[1],), F32))
            o = _swa_attention(q, kvdup, swa_sinks[j], t)
            xs = _proj_residual(o, swa_o_bf16, j, xs, gate2)
        xs = ffn(xs, i, 1, mod[6:9], last)
    return xs.reshape(b, t, d)
```

```python
import functools

import numpy as np
import jax
import jax.numpy as jnp
from jax import lax
from jax.experimental import pallas as pl
from jax.experimental.pallas import tpu as pltpu

F32 = jnp.float32
BF16 = jnp.bfloat16

NORM_EPS = 1e-6
FFN_RES = 0.5
MASKED = -1e30
MAX_FLOOR = -1e20
LOG2E = 1.4426950408889634

LANES = 128
V7X_VMEM_BYTES = 64 * 1024 * 1024

MOBA_HEADS = 16
MOBA_HEAD_DIM = 128
MOBA_BLOCK = 256
MOBA_TOPK = 3
MOBA_QTILE = 2 * MOBA_BLOCK
MOBA_KTILE = 2 * MOBA_BLOCK
MOBA_MAX_BLOCKS = 32

POOL_WINDOWS = (2, 4, 8, 16)
POOL_HALO = 16

SWA_HEAD_DIM = 64
SWA_Q_HEADS = 32
SWA_KV_HEADS = 4
SWA_GROUP = SWA_Q_HEADS // SWA_KV_HEADS
SWA_WINDOW = 128

NT_DIMS = (((1,), (1,)), ((), ()))


def _params(semantics, vmem_mib):
    return pltpu.CompilerParams(dimension_semantics=semantics,
                                vmem_limit_bytes=vmem_mib * 1024 * 1024)


def _norm_mod(xf, g, shift, scale):
    ms = jnp.mean(xf * xf, axis=-1, keepdims=True)
    y = xf * lax.rsqrt(ms + NORM_EPS)
    return y * (g * (1.0 + scale)) + shift


def _split_bf16(v):
    hi = v.astype(BF16).astype(F32)
    lo = (v - hi).astype(BF16).astype(F32)
    return hi, lo


def _ada_kernel(c_ref, w_ref, b_ref, o_ref):
    cb = c_ref[...]
    cs = cb * jax.nn.sigmoid(cb)
    w = w_ref[...]
    prod = w * jnp.concatenate([cs] * (w.shape[1] // LANES), axis=1)
    o_ref[...] = jnp.sum(prod, axis=0, keepdims=True) + b_ref[...]


def _ada_mod(c, ada_w, ada_b, tn=1024):
    depth, d, n = ada_w.shape
    c_rep = jnp.broadcast_to(c.reshape(d, 1), (d, LANES))
    out = pl.pallas_call(
        _ada_kernel,
        grid=(depth, n // tn),
        in_specs=[
            pl.BlockSpec((d, LANES), lambda l, j: (0, 0)),
            pl.BlockSpec((None, d, tn), lambda l, j: (l, 0, j)),
            pl.BlockSpec((None, 1, tn), lambda l, j: (l, 0, j)),
        ],
        out_specs=pl.BlockSpec((None, 1, tn), lambda l, j: (l, 0, j)),
        out_shape=jax.ShapeDtypeStruct((depth, 1, n), F32),
        compiler_params=_params(("arbitrary", "arbitrary"), 40),
        name="ada_mod",
    )(c_rep, ada_w, ada_b.reshape(depth, 1, n))
    return out.reshape(depth, n // d, d)


def _ffn_kernel(x_ref, g_ref, mod_ref, wg_ref, wu_ref, wd_ref, fg_ref, o_ref, h_ref, *, final_norm):
    j = pl.program_id(1)
    last = pl.num_programs(1) - 1

    def partial_out(h):
        gg = jnp.dot(h, wg_ref[...].astype(BF16), preferred_element_type=F32)
        uu = jnp.dot(h, wu_ref[...].astype(BF16), preferred_element_type=F32)
        a = ((gg * jax.nn.sigmoid(gg)) * uu).astype(BF16)
        return jnp.dot(a, wd_ref[...].astype(BF16), preferred_element_type=F32)

    @pl.when(j == 0)
    def _():
        h = _norm_mod(x_ref[...], g_ref[...], mod_ref[0:1, :], mod_ref[1:2, :]).astype(BF16)
        h_ref[...] = h
        o_ref[...] = partial_out(h)

    @pl.when(jnp.logical_and(j > 0, j < last))
    def _():
        o_ref[...] += partial_out(h_ref[...])

    @pl.when(j == last)
    def _():
        acc = o_ref[...] + partial_out(h_ref[...])
        out = x_ref[...] + (FFN_RES * mod_ref[2:3, :]) * acc
        if final_norm:
            ms = jnp.mean(out * out, axis=-1, keepdims=True)
            out = (out * lax.rsqrt(ms + NORM_EPS)) * fg_ref[...]
        o_ref[...] = out


def _ffn(x, g, mod3, wg, wu, wd, layer, slot, final_g, *, final_norm, tm=1024, tf=256):
    t, d = x.shape
    f = wg.shape[3]
    return pl.pallas_call(
        functools.partial(_ffn_kernel, final_norm=final_norm),
        grid=(t // tm, f // tf),
        in_specs=[
            pl.BlockSpec((tm, d), lambda i, j: (i, 0)),
            pl.BlockSpec((1, d), lambda i, j: (0, 0)),
            pl.BlockSpec((3, d), lambda i, j: (0, 0)),
            pl.BlockSpec((None, None, d, tf), lambda i, j: (layer, slot, 0, j)),
            pl.BlockSpec((None, None, d, tf), lambda i, j: (layer, slot, 0, j)),
            pl.BlockSpec((None, None, tf, d), lambda i, j: (layer, slot, j, 0)),
            pl.BlockSpec((1, d), lambda i, j: (0, 0)),
        ],
        out_specs=pl.BlockSpec((tm, d), lambda i, j: (i, 0)),
        out_shape=jax.ShapeDtypeStruct((t, d), F32),
        scratch_shapes=[pltpu.VMEM((tm, d), BF16)],
        compiler_params=_params(("arbitrary", "arbitrary"), 60),
        name="ffn",
    )(x, g.reshape(1, d), mod3, wg, wu, wd, final_g.reshape(1, d))


def _nmm_kernel(x_ref, g_ref, mod_ref, w_ref, cs_ref, o_ref, h_ref, *, head_major):
    def project(h):
        r = jnp.dot(h, w_ref[...].astype(BF16), preferred_element_type=F32)
        r = (r * cs_ref[...]).astype(o_ref.dtype)
        if head_major:
            for hh in range(o_ref.shape[0]):
                o_ref[hh] = r[:, hh * LANES:(hh + 1) * LANES]
        else:
            o_ref[...] = r

    @pl.when(pl.program_id(1) == 0)
    def _():
        h = _norm_mod(x_ref[...], g_ref[...], mod_ref[0:1, :], mod_ref[1:2, :]).astype(BF16)
        h_ref[...] = h
        project(h)

    @pl.when(pl.program_id(1) > 0)
    def _():
        project(h_ref[...])


def _norm_mod_matmul(x, g, mod3, w, widx, colscale, *, head_major=False, tm=1024, tn=1024):
    t, d = x.shape
    n = w.shape[2]
    if head_major:
        out_spec = pl.BlockSpec((tn // LANES, tm, LANES), lambda i, j: (j, i, 0))
        out_shape = jax.ShapeDtypeStruct((n // LANES, t, LANES), BF16)
    else:
        out_spec = pl.BlockSpec((tm, tn), lambda i, j: (i, j))
        out_shape = jax.ShapeDtypeStruct((t, n), BF16)
    return pl.pallas_call(
        functools.partial(_nmm_kernel, head_major=head_major),
        grid=(t // tm, n // tn),
        in_specs=[
            pl.BlockSpec((tm, d), lambda i, j: (i, 0)),
            pl.BlockSpec((1, d), lambda i, j: (0, 0)),
            pl.BlockSpec((3, d), lambda i, j: (0, 0)),
            pl.BlockSpec((None, d, tn), lambda i, j: (widx, 0, j)),
            pl.BlockSpec((1, tn), lambda i, j: (0, j)),
        ],
        out_specs=out_spec,
        out_shape=out_shape,
        scratch_shapes=[pltpu.VMEM((tm, d), BF16)],
        compiler_params=_params(("arbitrary", "arbitrary"), 48),
        name="norm_mod_matmul",
    )(x, g.reshape(1, d), mod3, w, colscale.reshape(1, n))


def _proj_res_kernel(a_ref, w_ref, x_ref, gate_ref, o_ref):
    r = jnp.dot(a_ref[...], w_ref[...], preferred_element_type=F32)
    o_ref[...] = x_ref[...] + gate_ref[...] * r


def _proj_residual(a, w, widx, x, gate, *, tm=512):
    t, k = a.shape
    d = w.shape[2]
    return pl.pallas_call(
        _proj_res_kernel,
        grid=(t // tm,),
        in_specs=[
            pl.BlockSpec((tm, k), lambda i: (i, 0)),
            pl.BlockSpec((None, k, d), lambda i: (widx, 0, 0)),
            pl.BlockSpec((tm, d), lambda i: (i, 0)),
            pl.BlockSpec((1, d), lambda i: (0, 0)),
        ],
        out_specs=pl.BlockSpec((tm, d), lambda i: (i, 0)),
        out_shape=jax.ShapeDtypeStruct((t, d), F32),
        compiler_params=_params(("arbitrary",), 48),
        name="proj_residual",
    )(a, w, x, gate)


def _moba_key_const(t):
    pos = np.arange(t)
    lane = np.arange(LANES)[None, :]
    blk = (pos // MOBA_BLOCK)[:, None]
    inblk = (pos % MOBA_BLOCK)[:, None].astype(np.float32)
    kc = np.where(lane < 64, (blk == (lane % 32)).astype(np.float32), 0.0)
    kc = np.where((lane == 64) | (lane == 65), 1.0, kc)
    kc = np.where((lane == 66) | (lane == 67), inblk, kc)
    return jnp.asarray(kc.T, dtype=BF16)


def _alibi_slopes(n):
    return np.asarray([2.0 ** (-8.0 * (i + 1) / n) for i in range(n)], np.float32)


def _moba_kernel(slope_ref, q_ref, k_ref, v_ref, kc_ref, o_ref,
                 kaug_t, vaug, qaug, kmean_s, s_ring, mb_ring, acc_s, *, nblk):
    dh = MOBA_HEAD_DIM
    tq, tk = MOBA_QTILE, MOBA_KTILE
    qt = pl.program_id(1)
    n_past = (qt * tq) // tk
    causal_shift = qt * tq - n_past * tk

    @pl.when(qt == 0)
    def _():
        k = k_ref[...]
        for j in range(kaug_t.shape[0]):
            kaug_t[j, 0:dh, :] = k[j * tk:(j + 1) * tk, :].astype(F32).T.astype(BF16)
            kaug_t[j, dh:2 * dh, :] = kc_ref[:, j * tk:(j + 1) * tk]
        vaug[:, 0:dh] = v_ref[...]
        ones_col = lax.broadcasted_iota(jnp.int32, (vaug.shape[0], dh), 1) == 0
        vaug[:, dh:2 * dh] = jnp.where(ones_col, 1.0, 0.0).astype(BF16)
        kf = k.astype(F32).reshape(nblk, MOBA_BLOCK, dh)
        kmean_s[...] = jnp.zeros(kmean_s.shape, F32)
        kmean_s[0:nblk, :] = jnp.sum(kf, axis=1) * (1.0 / MOBA_BLOCK)

    q = q_ref[...]
    km_hi, km_lo = _split_bf16(kmean_s[...])
    gate_t = (lax.dot_general(km_hi.astype(BF16), q, NT_DIMS, preferred_element_type=F32)
              + lax.dot_general(km_lo.astype(BF16), q, NT_DIMS, preferred_element_type=F32))
    blk_t = lax.broadcasted_iota(jnp.int32, (MOBA_MAX_BLOCKS, tq), 0)
    row_t = lax.broadcasted_iota(jnp.int32, (MOBA_MAX_BLOCKS, tq), 1)
    qblk_t = (tq // MOBA_BLOCK) * qt + row_t // MOBA_BLOCK
    blk_tf = blk_t.astype(F32)
    neg_inf = -jnp.inf

    g = jnp.where(blk_t < qblk_t, gate_t[0:MOBA_MAX_BLOCKS, :], neg_inf)
    act_t = jnp.where(blk_t == qblk_t, 1.0, 0.0)
    for _ in range(MOBA_TOPK):
        mx = jnp.max(g, axis=0, keepdims=True)
        idx = jnp.min(jnp.where(g == mx, blk_tf, float(LANES)), axis=0, keepdims=True)
        hit = blk_tf == jnp.where(mx > neg_inf, idx, float(LANES))
        act_t = jnp.where(hit, 1.0, act_t)
        g = jnp.where(hit, neg_inf, g)
    pad = jnp.zeros((LANES - 2 * MOBA_MAX_BLOCKS, tq), F32)
    active = jnp.concatenate([act_t, act_t, pad], axis=0).T > 0.5

    lane = lax.broadcasted_iota(jnp.int32, (tq, LANES), 1)
    row = lax.broadcasted_iota(jnp.int32, (tq, LANES), 0)
    blk = lane & (MOBA_MAX_BLOCKS - 1)
    slope = slope_ref[...]
    offset = -slope * (tq * qt - MOBA_BLOCK * blk).astype(F32)
    off_hi, off_lo = _split_bf16(offset)
    row_hi, row_lo = _split_bf16(-slope * row.astype(F32))
    sl_hi, sl_lo = _split_bf16(jnp.broadcast_to(slope, (tq, LANES)))
    extra = jnp.where(lane < 32, jnp.where(active, off_hi, MASKED),
            jnp.where(lane < 64, jnp.where(active, off_lo, 0.0),
            jnp.where(lane == 64, row_hi,
            jnp.where(lane == 65, row_lo,
            jnp.where(lane == 66, sl_hi,
            jnp.where(lane == 67, sl_lo, 0.0))))))
    qaug[:, 0:dh] = q
    qaug[:, dh:2 * dh] = extra.astype(BF16)

    def key_tile(u):
        return jnp.where(u == 0, n_past, jnp.maximum(jnp.minimum(u, n_past) - 1, 0))

    def scores(u):
        return jnp.dot(qaug[...], kaug_t[key_tile(u)], preferred_element_type=F32)

    def own_scores():
        r2 = lax.broadcasted_iota(jnp.int32, (tq, tk), 0)
        c2 = lax.broadcasted_iota(jnp.int32, (tq, tk), 1)
        return jnp.where(c2 <= r2 + causal_shift, scores(0), MASKED)

    def running_max(s, prev):
        r = jnp.max(s, axis=1, keepdims=True)
        return jnp.maximum(prev, jnp.broadcast_to(r, (tq, LANES)))

    floor = jnp.full((tq, LANES), MAX_FLOOR, F32)
    s0 = own_scores()
    s_ring[0] = s0
    mb_ring[0] = running_max(s0, floor)
    mb_ring[1] = floor
    s_ring[1] = scores(1)
    acc_s[...] = jnp.zeros(acc_s.shape, F32)

    def step(c, carry):
        cur, nxt, nxt2 = lax.rem(c, 3), lax.rem(c + 1, 3), lax.rem(c + 2, 3)
        mcur, mprev = lax.rem(c, 2), lax.rem(c + 1, 2)
        mb = mb_ring[mcur]
        alpha = jnp.exp2(mb_ring[mprev] - mb)
        tv = pl.multiple_of(key_tile(c) * tk, tk)
        vt = vaug[pl.ds(tv, tk), :]
        for lo in range(0, tq, tq // 2):
            rows = slice(lo, lo + tq // 2)
            mbh = mb[rows]
            p = jnp.exp2(s_ring[cur, rows, :] - jnp.concatenate([mbh] * (tk // LANES), axis=1)).astype(BF16)
            pv = jnp.dot(p, vt, preferred_element_type=F32)
            ah = alpha[rows]
            acc_s[rows, :] = acc_s[rows, :] * jnp.concatenate([ah, ah], axis=1) + pv
        mb_ring[mprev] = running_max(s_ring[nxt], mb)
        kt = kaug_t[key_tile(c + 2)]
        for lo in range(0, tq, tq // 2):
            rows = slice(lo, lo + tq // 2)
            s_ring[nxt2, rows, :] = jnp.dot(qaug[rows, :], kt, preferred_element_type=F32)
        return carry

    lax.fori_loop(0, n_past + 1, step, 0)

    acc = acc_s[...]
    o_ref[...] = (acc[:, 0:dh] / acc[:, dh:dh + 1]).astype(o_ref.dtype)


def _moba_attention(qkv, t):
    h, dh, tq, tk = MOBA_HEADS, MOBA_HEAD_DIM, MOBA_QTILE, MOBA_KTILE
    nblk = t // MOBA_BLOCK
    assert t % tk == 0 and tk % tq == 0 and nblk <= MOBA_MAX_BLOCKS
    slopes = _alibi_slopes(h).astype(np.float64) * LOG2E
    slopes = jnp.asarray(np.broadcast_to(slopes.astype(np.float32)[:, None, None], (h, 1, LANES)))
    return pl.pallas_call(
        functools.partial(_moba_kernel, nblk=nblk),
        grid=(h, t // tq),
        in_specs=[
            pl.BlockSpec((None, 1, LANES), lambda hh, p: (hh, 0, 0)),
            pl.BlockSpec((None, tq, dh), lambda hh, p: (hh, p, 0)),
            pl.BlockSpec((None, t, dh), lambda hh, p: (h + hh, 0, 0)),
            pl.BlockSpec((None, t, dh), lambda hh, p: (2 * h + hh, 0, 0)),
            pl.BlockSpec((LANES, t), lambda hh, p: (0, 0)),
        ],
        out_specs=pl.BlockSpec((tq, dh), lambda hh, p: (p, hh)),
        out_shape=jax.ShapeDtypeStruct((t, h * dh), BF16),
        scratch_shapes=[
            pltpu.VMEM((t // tk, 2 * dh, tk), BF16),
            pltpu.VMEM((t, 2 * dh), BF16),
            pltpu.VMEM((tq, 2 * dh), BF16),
            pltpu.VMEM((LANES, dh), F32),
            pltpu.VMEM((3, tq, tk), F32),
            pltpu.VMEM((2, tq, LANES), F32),
            pltpu.VMEM((tq, 2 * dh), F32),
        ],
        compiler_params=_params(("arbitrary", "arbitrary"), 48),
        name="moba_attention",
    )(slopes, qkv, qkv, qkv, _moba_key_const(t))


def _pool_kernel(x_ref, xh_ref, g_ref, mod_ref, w_ref, ps_ref, o_ref, hbuf):
    i = pl.program_id(0)
    tm = x_ref.shape[0]
    grp = w_ref.shape[1]
    g, shift, scale, gate = g_ref[...], mod_ref[0:1, :], mod_ref[1:2, :], mod_ref[2:3, :]
    x = x_ref[...]
    h = _norm_mod(x, g, shift, scale)
    halo = _norm_mod(xh_ref[...], g, shift, scale)
    hbuf[0:POOL_HALO, :] = jnp.where(i == 0, 0.0, halo)
    hbuf[POOL_HALO:POOL_HALO + tm, :] = h
    tpos = i * tm + lax.broadcasted_iota(jnp.int32, (tm, 1), 0)
    for gi, win in enumerate(POOL_WINDOWS):
        cols = slice(gi * grp, (gi + 1) * grp)
        hg = h[:, cols]
        acc = hg
        for back in range(1, win):
            acc = acc + hbuf[POOL_HALO - back:POOL_HALO - back + tm, cols]
        cnt = jnp.minimum(tpos + 1, win).astype(F32)
        pooled = acc / cnt - hg
        mixed = jnp.dot(pooled.astype(BF16), w_ref[gi], preferred_element_type=F32)
        o_ref[:, cols] = x[:, cols] + gate[:, cols] * (mixed * ps_ref[:, cols])


def _pool_mixer(x, g, mod3, w_pool, pool_scale, *, tm=512):
    t, d = x.shape
    ng, grp, _ = w_pool.shape
    halo_blocks = tm // POOL_HALO
    return pl.pallas_call(
        _pool_kernel,
        grid=(t // tm,),
        in_specs=[
            pl.BlockSpec((tm, d), lambda i: (i, 0)),
            pl.BlockSpec((POOL_HALO, d), lambda i: (jnp.maximum(i * halo_blocks - 1, 0), 0)),
            pl.BlockSpec((1, d), lambda i: (0, 0)),
            pl.BlockSpec((3, d), lambda i: (0, 0)),
            pl.BlockSpec((ng, grp, grp), lambda i: (0, 0, 0)),
            pl.BlockSpec((1, d), lambda i: (0, 0)),
        ],
        out_specs=pl.BlockSpec((tm, d), lambda i: (i, 0)),
        out_shape=jax.ShapeDtypeStruct((t, d), F32),
        scratch_shapes=[pltpu.VMEM((POOL_HALO + tm, d), F32)],
        compiler_params=_params(("arbitrary",), 48),
        name="pool_mixer",
    )(x, x, g.reshape(1, d), mod3, w_pool, pool_scale.reshape(1, d))


def _swa_bias_const():
    w = SWA_WINDOW
    r = np.arange(w)[:, None]
    c = np.arange(2 * w)[None, :]
    dist = (r + w - c).astype(np.float32)
    valid = (dist >= 0) & (dist < w)
    slopes = _alibi_slopes(SWA_Q_HEADS)
    npair = SWA_GROUP // 2
    out = np.empty((SWA_KV_HEADS, npair * w, 2 * 2 * w), np.float32)
    for kv in range(SWA_KV_HEADS):
        for p in range(npair):
            for half in range(2):
                sl = slopes[kv * SWA_GROUP + 2 * p + half]
                tile = np.where(valid, -sl * dist, np.float32(MASKED)).astype(np.float32)
                out[kv, p * w:(p + 1) * w, half * 2 * w:(half + 1) * 2 * w] = tile
    return jnp.asarray(out)


def _swa_kernel(q_ref, kp_ref, kc_ref, vp_ref, vc_ref, bias_ref, sink_ref, o_ref):
    n = pl.program_id(0)
    w = SWA_WINDOW
    npair = SWA_GROUP // 2
    rows = npair * w
    lane = lax.broadcasted_iota(jnp.int32, (2 * w, LANES), 1)
    low = lane < SWA_HEAD_DIM
    col = lax.broadcasted_iota(jnp.int32, (rows, 4 * w), 1)
    prev_cols = (col & (2 * w - 1)) < w
    no_prev = jnp.logical_and(n == 0, prev_cols)
    out_low = lax.broadcasted_iota(jnp.int32, (rows, LANES), 1) < SWA_HEAD_DIM
    ones_even = jnp.where(low, 1.0, 0.0)
    ones_blk = jnp.concatenate([ones_even, 1.0 - ones_even], axis=0)
    for kv in range(SWA_KV_HEADS):
        ksl = slice(kv * LANES, (kv + 1) * LANES)
        kd = jnp.concatenate([kp_ref[:, ksl], kc_ref[:, ksl]], axis=0).astype(F32)
        vd = jnp.concatenate([vp_ref[:, ksl], vc_ref[:, ksl]], axis=0).astype(F32)
        kk = jnp.concatenate([jnp.where(low, kd, 0.0), jnp.where(low, 0.0, kd)], axis=0).astype(BF16)
        vv = jnp.concatenate([jnp.where(low, vd, 0.0), jnp.where(low, 0.0, vd)], axis=0)
        vv_aug = jnp.concatenate([vv, ones_blk], axis=1).astype(BF16)
        base = kv * npair * LANES
        qs = jnp.concatenate(
            [q_ref[:, base + p * LANES:base + (p + 1) * LANES] for p in range(npair)], axis=0)
        s = lax.dot_general(qs, kk, NT_DIMS, preferred_element_type=F32) + bias_ref[kv]
        s = jnp.where(no_prev, MASKED, s)
        es, ms = [], []
        for half in range(2):
            sh = s[:, half * 2 * w:(half + 1) * 2 * w]
            r = jnp.max(sh, axis=1, keepdims=True)
            m = jnp.maximum(jnp.broadcast_to(r, (rows, LANES)), sink_ref[kv, half])
            es.append(jnp.exp(sh - jnp.concatenate([m, m], axis=1)))
            ms.append(m)
        p = jnp.concatenate(es, axis=1).astype(BF16)
        o = jnp.dot(p, vv_aug, preferred_element_type=F32)
        m_mix = jnp.where(out_low, ms[0], ms[1])
        sink_mix = jnp.where(out_low, sink_ref[kv, 0], sink_ref[kv, 1])
        o = o[:, 0:LANES] / (o[:, LANES:2 * LANES] + jnp.exp(sink_mix - m_mix))
        for pp in range(npair):
            o_ref[:, base + pp * LANES:base + (pp + 1) * LANES] = (
                o[pp * w:(pp + 1) * w, :].astype(o_ref.dtype))


def _swa_attention(q, kvdup, sinks, t):
    w = SWA_WINDOW
    dq = SWA_Q_HEADS * SWA_HEAD_DIM
    dkv = SWA_KV_HEADS * LANES
    npair = SWA_GROUP // 2
    sink_tab = jnp.broadcast_to(
        sinks.astype(F32).reshape(SWA_KV_HEADS, npair, 2).transpose(0, 2, 1)[:, :, :, None, None],
        (SWA_KV_HEADS, 2, npair, w, LANES)).reshape(SWA_KV_HEADS, 2, npair * w, LANES)
    prev = lambda n: jnp.maximum(n - 1, 0)
    return pl.pallas_call(
        _swa_kernel,
        grid=(t // w,),
        in_specs=[
            pl.BlockSpec((w, dq), lambda n: (n, 0)),
            pl.BlockSpec((w, dkv), lambda n: (prev(n), 0)),
            pl.BlockSpec((w, dkv), lambda n: (n, 0)),
            pl.BlockSpec((w, dkv), lambda n: (prev(n), 1)),
            pl.BlockSpec((w, dkv), lambda n: (n, 1)),
            pl.BlockSpec((SWA_KV_HEADS, npair * w, 4 * w), lambda n: (0, 0, 0)),
            pl.BlockSpec((SWA_KV_HEADS, 2, npair * w, LANES), lambda n: (0, 0, 0, 0)),
        ],
        out_specs=pl.BlockSpec((w, dq), lambda n: (n, 0)),
        out_shape=jax.ShapeDtypeStruct((t, dq), BF16),
        compiler_params=_params(("arbitrary",), 48),
        name="swa_attention",
    )(q, kvdup, kvdup, kvdup, kvdup, _swa_bias_const(), sink_tab)


def kernel(x, c, norm_g, ada_w, ada_b, ffn_w_gate, ffn_w_up, ffn_w_down, moba_w_qkv, moba_w_o,
           pool_w, pool_scale, swa_w_qkv, swa_w_o, swa_sinks, final_g):
    b, t, d = x.shape
    assert b == 1
    depth = norm_g.shape[0]
    mods = _ada_mod(c, ada_w, ada_b)

    wg, wu, wd = ffn_w_gate, ffn_w_up, ffn_w_down
    moba_o_bf16 = moba_w_o.astype(BF16)
    swa_o_bf16 = swa_w_o.astype(BF16)

    def ffn(xs, i, s, mod3, final_norm):
        return _ffn(xs, norm_g[i, 2 * s], mod3, wg, wu, wd, i, s, final_g, final_norm=final_norm)

    xs = x.reshape(t, d)
    for i in range(depth):
        mod = mods[i]
        last = i == depth - 1
        xs = ffn(xs, i, 0, mod[0:3], False)
        kind, j = i % 3, i // 3
        m2 = mod[3:6]
        gate2 = mod[5:6]
        if kind == 0:
            qscale = jnp.concatenate([jnp.full((d,), MOBA_HEAD_DIM ** -0.5 * LOG2E, F32), jnp.ones((2 * d,), F32)])
            qkv = _norm_mod_matmul(xs, norm_g[i, 1], m2, moba_w_qkv, j, qscale, head_major=True)
            o = _moba_attention(qkv, t)
            xs = _proj_residual(o, moba_o_bf16, j, xs, gate2)
        elif kind == 1:
            xs = _pool_mixer(xs, norm_g[i, 1], m2, pool_w[j].astype(BF16), pool_scale[j])
        else:
            dq = SWA_Q_HEADS * SWA_HEAD_DIM
            hd = SWA_HEAD_DIM
            wqkv = swa_w_qkv[j]
            wq = wqkv[:, :dq]
            wk = wqkv[:, dq:dq + SWA_KV_HEADS * hd].reshape(d, SWA_KV_HEADS, 1, hd)
            wv = wqkv[:, dq + SWA_KV_HEADS * hd:].reshape(d, SWA_KV_HEADS, 1, hd)
            dup = lambda a: jnp.broadcast_to(a, (d, SWA_KV_HEADS, 2, hd)).reshape(d, SWA_KV_HEADS * 2 * hd)
            wkv = jnp.concatenate([dup(wk), dup(wv)], axis=1)
            q = _norm_mod_matmul(xs, norm_g[i, 1], m2, wq[None], 0,
                                 jnp.full((dq,), hd ** -0.5, F32))
            kvdup = _norm_mod_matmul(xs, norm_g[i, 1], m2, wkv[None], 0, jnp.ones((wkv.shape[1],), F32))
            o = _swa_attention(q, kvdup, swa_sinks[j], t)
            xs = _proj_residual(o, swa_o_bf16, j, xs, gate2)
        xs = ffn(xs, i, 1, mod[6:9], last)
    return xs.reshape(b, t, d)
```

```python
import functools

import numpy as np
import jax
import jax.numpy as jnp
from jax import lax
from jax.experimental import pallas as pl
from jax.experimental.pallas import tpu as pltpu

F32 = jnp.float32
BF16 = jnp.bfloat16

NORM_EPS = 1e-6
FFN_RES = 0.5
MASKED = -1e30
MAX_FLOOR = -1e20
LOG2E = 1.4426950408889634

LANES = 128
V7X_VMEM_BYTES = 64 * 1024 * 1024

MOBA_HEADS = 16
MOBA_HEAD_DIM = 128
MOBA_BLOCK = 256
MOBA_TOPK = 3
MOBA_QTILE = 2 * MOBA_BLOCK
MOBA_KTILE = 2 * MOBA_BLOCK
MOBA_MAX_BLOCKS = 32

POOL_WINDOWS = (2, 4, 8, 16)
POOL_HALO = 16

SWA_HEAD_DIM = 64
SWA_Q_HEADS = 32
SWA_KV_HEADS = 4
SWA_GROUP = SWA_Q_HEADS // SWA_KV_HEADS
SWA_WINDOW = 128

NT_DIMS = (((1,), (1,)), ((), ()))


def _params(semantics, vmem_mib):
    return pltpu.CompilerParams(dimension_semantics=semantics,
                                vmem_limit_bytes=vmem_mib * 1024 * 1024)


def _norm_mod(xf, g, shift, scale):
    ms = jnp.mean(xf * xf, axis=-1, keepdims=True)
    y = xf * lax.rsqrt(ms + NORM_EPS)
    return y * (g * (1.0 + scale)) + shift


def _split_bf16(v):
    hi = v.astype(BF16).astype(F32)
    lo = (v - hi).astype(BF16).astype(F32)
    return hi, lo


def _ada_kernel(c_ref, w_ref, b_ref, o_ref):
    cb = c_ref[...]
    cs = cb * jax.nn.sigmoid(cb)
    w = w_ref[...]
    prod = w * jnp.concatenate([cs] * (w.shape[1] // LANES), axis=1)
    o_ref[...] = jnp.sum(prod, axis=0, keepdims=True) + b_ref[...]


def _ada_mod(c, ada_w, ada_b, tn=1024):
    depth, d, n = ada_w.shape
    c_rep = jnp.broadcast_to(c.reshape(d, 1), (d, LANES))
    out = pl.pallas_call(
        _ada_kernel,
        grid=(depth, n // tn),
        in_specs=[
            pl.BlockSpec((d, LANES), lambda l, j: (0, 0)),
            pl.BlockSpec((None, d, tn), lambda l, j: (l, 0, j)),
            pl.BlockSpec((None, 1, tn), lambda l, j: (l, 0, j)),
        ],
        out_specs=pl.BlockSpec((None, 1, tn), lambda l, j: (l, 0, j)),
        out_shape=jax.ShapeDtypeStruct((depth, 1, n), F32),
        compiler_params=_params(("arbitrary", "arbitrary"), 40),
        name="ada_mod",
    )(c_rep, ada_w, ada_b.reshape(depth, 1, n))
    return out.reshape(depth, n // d, d)


def _ffn_kernel(x_ref, g_ref, mod_ref, wg_ref, wu_ref, wd_ref, fg_ref, o_ref, h_ref, *, final_norm):
    j = pl.program_id(1)
    last = pl.num_programs(1) - 1

    def partial_out(h):
        gg = jnp.dot(h, wg_ref[...].astype(BF16), preferred_element_type=F32)
        uu = jnp.dot(h, wu_ref[...].astype(BF16), preferred_element_type=F32)
        a = ((gg * jax.nn.sigmoid(gg)) * uu).astype(BF16)
        return jnp.dot(a, wd_ref[...].astype(BF16), preferred_element_type=F32)

    @pl.when(j == 0)
    def _():
        h = _norm_mod(x_ref[...], g_ref[...], mod_ref[0:1, :], mod_ref[1:2, :]).astype(BF16)
        h_ref[...] = h
        o_ref[...] = partial_out(h)

    @pl.when(jnp.logical_and(j > 0, j < last))
    def _():
        o_ref[...] += partial_out(h_ref[...])

    @pl.when(j == last)
    def _():
        acc = o_ref[...] + partial_out(h_ref[...])
        out = x_ref[...] + (FFN_RES * mod_ref[2:3, :]) * acc
        if final_norm:
            ms = jnp.mean(out * out, axis=-1, keepdims=True)
            out = (out * lax.rsqrt(ms + NORM_EPS)) * fg_ref[...]
        o_ref[...] = out


def _ffn(x, g, mod3, wg, wu, wd, layer, slot, final_g, *, final_norm, tm=1024, tf=256):
    t, d = x.shape
    f = wg.shape[3]
    return pl.pallas_call(
        functools.partial(_ffn_kernel, final_norm=final_norm),
        grid=(t // tm, f // tf),
        in_specs=[
            pl.BlockSpec((tm, d), lambda i, j: (i, 0)),
            pl.BlockSpec((1, d), lambda i, j: (0, 0)),
            pl.BlockSpec((3, d), lambda i, j: (0, 0)),
            pl.BlockSpec((None, None, d, tf), lambda i, j: (layer, slot, 0, j)),
            pl.BlockSpec((None, None, d, tf), lambda i, j: (layer, slot, 0, j)),
            pl.BlockSpec((None, None, tf, d), lambda i, j: (layer, slot, j, 0)),
            pl.BlockSpec((1, d), lambda i, j: (0, 0)),
        ],
        out_specs=pl.BlockSpec((tm, d), lambda i, j: (i, 0)),
        out_shape=jax.ShapeDtypeStruct((t, d), F32),
        scratch_shapes=[pltpu.VMEM((tm, d), BF16)],
        compiler_params=_params(("arbitrary", "arbitrary"), 60),
        name="ffn",
    )(x, g.reshape(1, d), mod3, wg, wu, wd, final_g.reshape(1, d))


def _nmm_kernel(x_ref, g_ref, mod_ref, w_ref, cs_ref, o_ref, h_ref, *, head_major):
    def project(h):
        r = jnp.dot(h, w_ref[...].astype(BF16), preferred_element_type=F32)
        r = (r * cs_ref[...]).astype(o_ref.dtype)
        if head_major:
            for hh in range(o_ref.shape[0]):
                o_ref[hh] = r[:, hh * LANES:(hh + 1) * LANES]
        else:
            o_ref[...] = r

    @pl.when(pl.program_id(1) == 0)
    def _():
        h = _norm_mod(x_ref[...], g_ref[...], mod_ref[0:1, :], mod_ref[1:2, :]).astype(BF16)
        h_ref[...] = h
        project(h)

    @pl.when(pl.program_id(1) > 0)
    def _():
        project(h_ref[...])


def _norm_mod_matmul(x, g, mod3, w, widx, colscale, *, head_major=False, tm=1024, tn=1024):
    t, d = x.shape
    n = w.shape[2]
    if head_major:
        out_spec = pl.BlockSpec((tn // LANES, tm, LANES), lambda i, j: (j, i, 0))
        out_shape = jax.ShapeDtypeStruct((n // LANES, t, LANES), BF16)
    else:
        out_spec = pl.BlockSpec((tm, tn), lambda i, j: (i, j))
        out_shape = jax.ShapeDtypeStruct((t, n), BF16)
    return pl.pallas_call(
        functools.partial(_nmm_kernel, head_major=head_major),
        grid=(t // tm, n // tn),
        in_specs=[
            pl.BlockSpec((tm, d), lambda i, j: (i, 0)),
            pl.BlockSpec((1, d), lambda i, j: (0, 0)),
            pl.BlockSpec((3, d), lambda i, j: (0, 0)),
            pl.BlockSpec((None, d, tn), lambda i, j: (widx, 0, j)),
            pl.BlockSpec((1, tn), lambda i, j: (0, j)),
        ],
        out_specs=out_spec,
        out_shape=out_shape,
        scratch_shapes=[pltpu.VMEM((tm, d), BF16)],
        compiler_params=_params(("arbitrary", "arbitrary"), 48),
        name="norm_mod_matmul",
    )(x, g.reshape(1, d), mod3, w, colscale.reshape(1, n))


def _proj_res_kernel(a_ref, w_ref, x_ref, gate_ref, o_ref):
    r = jnp.dot(a_ref[...], w_ref[...], preferred_element_type=F32)
    o_ref[...] = x_ref[...] + gate_ref[...] * r


def _proj_residual(a, w, widx, x, gate, *, tm=512):
    t, k = a.shape
    d = w.shape[2]
    return pl.pallas_call(
        _proj_res_kernel,
        grid=(t // tm,),
        in_specs=[
            pl.BlockSpec((tm, k), lambda i: (i, 0)),
            pl.BlockSpec((None, k, d), lambda i: (widx, 0, 0)),
            pl.BlockSpec((tm, d), lambda i: (i, 0)),
            pl.BlockSpec((1, d), lambda i: (0, 0)),
        ],
        out_specs=pl.BlockSpec((tm, d), lambda i: (i, 0)),
        out_shape=jax.ShapeDtypeStruct((t, d), F32),
        compiler_params=_params(("arbitrary",), 48),
        name="proj_residual",
    )(a, w, x, gate)


def _moba_key_const(t):
    pos = np.arange(t)
    lane = np.arange(LANES)[None, :]
    blk = (pos // MOBA_BLOCK)[:, None]
    inblk = (pos % MOBA_BLOCK)[:, None].astype(np.float32)
    kc = np.where(lane < 64, (blk == (lane % 32)).astype(np.float32), 0.0)
    kc = np.where((lane == 64) | (lane == 65), 1.0, kc)
    kc = np.where((lane == 66) | (lane == 67), inblk, kc)
    return jnp.asarray(kc.T, dtype=BF16)


def _alibi_slopes(n):
    return np.asarray([2.0 ** (-8.0 * (i + 1) / n) for i in range(n)], np.float32)


def _moba_kernel(slope_ref, q_ref, k_ref, v_ref, kc_ref, o_ref,
                 kaug_t, vaug, qaug, kmean_s, s_ring, mb_ring, acc_s, *, nblk):
    dh = MOBA_HEAD_DIM
    tq, tk = MOBA_QTILE, MOBA_KTILE
    qt = pl.program_id(1)
    n_past = (qt * tq) // tk
    n_diag = max(tq // tk, 1)
    causal_shift = qt * tq - n_past * tk

    @pl.when(qt == 0)
    def _():
        k = k_ref[...]
        for j in range(kaug_t.shape[0]):
            kaug_t[j, 0:dh, :] = k[j * tk:(j + 1) * tk, :].astype(F32).T.astype(BF16)
            kaug_t[j, dh:2 * dh, :] = kc_ref[:, j * tk:(j + 1) * tk]
        vaug[:, 0:dh] = v_ref[...]
        vaug[:, dh:2 * dh] = jnp.ones((vaug.shape[0], dh), BF16)
        kf = k.astype(F32).reshape(nblk, MOBA_BLOCK, dh)
        kmean_s[...] = jnp.zeros(kmean_s.shape, F32)
        kmean_s[0:nblk, :] = jnp.sum(kf, axis=1) * (1.0 / MOBA_BLOCK)

    q = q_ref[...]
    km_hi, km_lo = _split_bf16(kmean_s[...])
    gate_t = (lax.dot_general(km_hi.astype(BF16), q, NT_DIMS, preferred_element_type=F32)
              + lax.dot_general(km_lo.astype(BF16), q, NT_DIMS, preferred_element_type=F32))
    blk_t = lax.broadcasted_iota(jnp.int32, (MOBA_MAX_BLOCKS, tq), 0)
    row_t = lax.broadcasted_iota(jnp.int32, (MOBA_MAX_BLOCKS, tq), 1)
    qblk_t = (tq // MOBA_BLOCK) * qt + row_t // MOBA_BLOCK
    blk_tf = blk_t.astype(F32)
    neg_inf = -jnp.inf

    g = jnp.where(blk_t < qblk_t, gate_t[0:MOBA_MAX_BLOCKS, :], neg_inf)
    act_t = jnp.where(blk_t == qblk_t, 1.0, 0.0)
    for _ in range(MOBA_TOPK):
        mx = jnp.max(g, axis=0, keepdims=True)
        idx = jnp.min(jnp.where(g == mx, blk_tf, float(LANES)), axis=0, keepdims=True)
        hit = blk_tf == jnp.where(mx > neg_inf, idx, float(LANES))
        act_t = jnp.where(hit, 1.0, act_t)
        g = jnp.where(hit, neg_inf, g)
    pad = jnp.zeros((LANES - 2 * MOBA_MAX_BLOCKS, tq), F32)
    active = jnp.concatenate([act_t, act_t, pad], axis=0).T > 0.5

    lane = lax.broadcasted_iota(jnp.int32, (tq, LANES), 1)
    row = lax.broadcasted_iota(jnp.int32, (tq, LANES), 0)
    blk = lane & (MOBA_MAX_BLOCKS - 1)
    slope = slope_ref[...]
    offset = -slope * (tq * qt - MOBA_BLOCK * blk).astype(F32)
    off_hi, off_lo = _split_bf16(offset)
    row_hi, row_lo = _split_bf16(-slope * row.astype(F32))
    sl_hi, sl_lo = _split_bf16(jnp.broadcast_to(slope, (tq, LANES)))
    extra = jnp.where(lane < 32, jnp.where(active, off_hi, MASKED),
            jnp.where(lane < 64, jnp.where(active, off_lo, 0.0),
            jnp.where(lane == 64, row_hi,
            jnp.where(lane == 65, row_lo,
            jnp.where(lane == 66, sl_hi,
            jnp.where(lane == 67, sl_lo, 0.0))))))
    qaug[:, 0:dh] = q
    qaug[:, dh:2 * dh] = extra.astype(BF16)

    def key_tile(u):
        if n_diag == 1:
            return jnp.where(u == 0, n_past, jnp.maximum(jnp.minimum(u, n_past) - 1, 0))
        past = jnp.maximum(jnp.minimum(u, n_past + n_diag - 1) - n_diag, 0)
        return jnp.where(u < n_diag, n_past + u, past)

    def scores(u):
        return jnp.dot(qaug[...], kaug_t[key_tile(u)], preferred_element_type=F32)

    def diag_scores(u):
        r2 = lax.broadcasted_iota(jnp.int32, (tq, tk), 0)
        c2 = lax.broadcasted_iota(jnp.int32, (tq, tk), 1)
        return jnp.where(c2 + u * tk <= r2 + causal_shift, scores(u), MASKED)

    def running_max(s, prev):
        r = jnp.max(s, axis=1, keepdims=True)
        return jnp.maximum(prev, jnp.broadcast_to(r, (tq, LANES)))

    floor = jnp.full((tq, LANES), MAX_FLOOR, F32)
    assert n_diag <= 2
    s0 = diag_scores(0)
    s_ring[0] = s0
    mb_ring[0] = running_max(s0, floor)
    mb_ring[1] = floor
    s_ring[1] = diag_scores(1) if n_diag == 2 else scores(1)
    acc_s[...] = jnp.zeros(acc_s.shape, F32)

    def step(c, carry):
        cur, nxt, nxt2 = lax.rem(c, 3), lax.rem(c + 1, 3), lax.rem(c + 2, 3)
        mcur, mprev = lax.rem(c, 2), lax.rem(c + 1, 2)
        mb = mb_ring[mcur]
        alpha = jnp.exp2(mb_ring[mprev] - mb)
        tv = pl.multiple_of(key_tile(c) * tk, tk)
        vt = vaug[pl.ds(tv, tk), :]
        for lo in range(0, tq, MOBA_BLOCK):
            rows = slice(lo, lo + MOBA_BLOCK)
            mbh = mb[rows]
            p = jnp.exp2(s_ring[cur, rows, :] - jnp.concatenate([mbh] * (tk // LANES), axis=1)).astype(BF16)
            pv = jnp.dot(p, vt, preferred_element_type=F32)
            ah = alpha[rows]
            acc_s[rows, :] = acc_s[rows, :] * jnp.concatenate([ah, ah], axis=1) + pv
        mb_ring[mprev] = running_max(s_ring[nxt], mb)
        kt = kaug_t[key_tile(c + 2)]
        for lo in range(0, tq, MOBA_BLOCK):
            rows = slice(lo, lo + MOBA_BLOCK)
            s_ring[nxt2, rows, :] = jnp.dot(qaug[rows, :], kt, preferred_element_type=F32)
        return carry

    lax.fori_loop(0, n_past + n_diag, step, 0)

    acc = acc_s[...]
    o_ref[...] = (acc[:, 0:dh] / acc[:, dh:2 * dh]).astype(o_ref.dtype)


def _moba_attention(qkv, t):
    h, dh, tq, tk = MOBA_HEADS, MOBA_HEAD_DIM, MOBA_QTILE, MOBA_KTILE
    nblk = t // MOBA_BLOCK
    assert t % tk == 0 and t % tq == 0 and (tk % tq == 0 or tq % tk == 0) and nblk <= MOBA_MAX_BLOCKS
    slopes = _alibi_slopes(h).astype(np.float64) * LOG2E
    slopes = jnp.asarray(np.broadcast_to(slopes.astype(np.float32)[:, None, None], (h, 1, LANES)))
    return pl.pallas_call(
        functools.partial(_moba_kernel, nblk=nblk),
        grid=(h, t // tq),
        in_specs=[
            pl.BlockSpec((None, 1, LANES), lambda hh, p: (hh, 0, 0)),
            pl.BlockSpec((None, tq, dh), lambda hh, p: (hh, p, 0)),
            pl.BlockSpec((None, t, dh), lambda hh, p: (h + hh, 0, 0)),
            pl.BlockSpec((None, t, dh), lambda hh, p: (2 * h + hh, 0, 0)),
            pl.BlockSpec((LANES, t), lambda hh, p: (0, 0)),
        ],
        out_specs=pl.BlockSpec((tq, dh), lambda hh, p: (p, hh)),
        out_shape=jax.ShapeDtypeStruct((t, h * dh), BF16),
        scratch_shapes=[
            pltpu.VMEM((t // tk, 2 * dh, tk), BF16),
            pltpu.VMEM((t, 2 * dh), BF16),
            pltpu.VMEM((tq, 2 * dh), BF16),
            pltpu.VMEM((LANES, dh), F32),
            pltpu.VMEM((3, tq, tk), F32),
            pltpu.VMEM((2, tq, LANES), F32),
            pltpu.VMEM((tq, 2 * dh), F32),
        ],
        compiler_params=_params(("arbitrary", "arbitrary"), 48),
        name="moba_attention",
    )(slopes, qkv, qkv, qkv, _moba_key_const(t))


def _pool_kernel(x_ref, xh_ref, g_ref, mod_ref, w_ref, ps_ref, o_ref, hbuf):
    i = pl.program_id(0)
    tm = x_ref.shape[0]
    grp = w_ref.shape[1]
    g, shift, scale, gate = g_ref[...], mod_ref[0:1, :], mod_ref[1:2, :], mod_ref[2:3, :]
    x = x_ref[...]
    h = _norm_mod(x, g, shift, scale)
    halo = _norm_mod(xh_ref[...], g, shift, scale)
    hbuf[0:POOL_HALO, :] = jnp.where(i == 0, 0.0, halo)
    hbuf[POOL_HALO:POOL_HALO + tm, :] = h
    tpos = i * tm + lax.broadcasted_iota(jnp.int32, (tm, 1), 0)
    for gi, win in enumerate(POOL_WINDOWS):
        cols = slice(gi * grp, (gi + 1) * grp)
        hg = h[:, cols]
        acc = hg
        for back in range(1, win):
            acc = acc + hbuf[POOL_HALO - back:POOL_HALO - back + tm, cols]
        cnt = jnp.minimum(tpos + 1, win).astype(F32)
        pooled = acc / cnt - hg
        mixed = jnp.dot(pooled.astype(BF16), w_ref[gi], preferred_element_type=F32)
        o_ref[:, cols] = x[:, cols] + gate[:, cols] * (mixed * ps_ref[:, cols])


def _pool_mixer(x, g, mod3, w_pool, pool_scale, *, tm=512):
    t, d = x.shape
    ng, grp, _ = w_pool.shape
    halo_blocks = tm // POOL_HALO
    return pl.pallas_call(
        _pool_kernel,
        grid=(t // tm,),
        in_specs=[
            pl.BlockSpec((tm, d), lambda i: (i, 0)),
            pl.BlockSpec((POOL_HALO, d), lambda i: (jnp.maximum(i * halo_blocks - 1, 0), 0)),
            pl.BlockSpec((1, d), lambda i: (0, 0)),
            pl.BlockSpec((3, d), lambda i: (0, 0)),
            pl.BlockSpec((ng, grp, grp), lambda i: (0, 0, 0)),
            pl.BlockSpec((1, d), lambda i: (0, 0)),
        ],
        out_specs=pl.BlockSpec((tm, d), lambda i: (i, 0)),
        out_shape=jax.ShapeDtypeStruct((t, d), F32),
        scratch_shapes=[pltpu.VMEM((POOL_HALO + tm, d), F32)],
        compiler_params=_params(("arbitrary",), 48),
        name="pool_mixer",
    )(x, x, g.reshape(1, d), mod3, w_pool, pool_scale.reshape(1, d))


def _swa_bias_const():
    w = SWA_WINDOW
    r = np.arange(w)[:, None]
    c = np.arange(2 * w)[None, :]
    dist = (r + w - c).astype(np.float32)
    valid = (dist >= 0) & (dist < w)
    slopes = _alibi_slopes(SWA_Q_HEADS)
    npair = SWA_GROUP // 2
    out = np.empty((SWA_KV_HEADS, npair * w, 2 * 2 * w), np.float32)
    for kv in range(SWA_KV_HEADS):
        for p in range(npair):
            for half in range(2):
                sl = slopes[kv * SWA_GROUP + 2 * p + half]
                tile = np.where(valid, -sl * dist, np.float32(MASKED)).astype(np.float32)
                out[kv, p * w:(p + 1) * w, half * 2 * w:(half + 1) * 2 * w] = tile
    return jnp.asarray(out)


def _swa_kernel(q_ref, kp_ref, kc_ref, vp_ref, vc_ref, bias_ref, sink_ref, o_ref):
    n = pl.program_id(0)
    w = SWA_WINDOW
    npair = SWA_GROUP // 2
    rows = npair * w
    lane = lax.broadcasted_iota(jnp.int32, (2 * w, LANES), 1)
    low = lane < SWA_HEAD_DIM
    col = lax.broadcasted_iota(jnp.int32, (rows, 4 * w), 1)
    prev_cols = (col & (2 * w - 1)) < w
    no_prev = jnp.logical_and(n == 0, prev_cols)
    out_low = lax.broadcasted_iota(jnp.int32, (rows, LANES), 1) < SWA_HEAD_DIM
    ones_even = jnp.where(low, 1.0, 0.0)
    ones_blk = jnp.concatenate([ones_even, 1.0 - ones_even], axis=0)
    for kv in range(SWA_KV_HEADS):
        ksl = slice(kv * LANES, (kv + 1) * LANES)
        kd = jnp.concatenate([kp_ref[:, ksl], kc_ref[:, ksl]], axis=0).astype(F32)
        vd = jnp.concatenate([vp_ref[:, ksl], vc_ref[:, ksl]], axis=0).astype(F32)
        kk = jnp.concatenate([jnp.where(low, kd, 0.0), jnp.where(low, 0.0, kd)], axis=0).astype(BF16)
        vv = jnp.concatenate([jnp.where(low, vd, 0.0), jnp.where(low, 0.0, vd)], axis=0)
        vv_aug = jnp.concatenate([vv, ones_blk], axis=1).astype(BF16)
        base = kv * npair * LANES
        qs = jnp.concatenate(
            [q_ref[:, base + p * LANES:base + (p + 1) * LANES] for p in range(npair)], axis=0)
        s = lax.dot_general(qs, kk, NT_DIMS, preferred_element_type=F32) + bias_ref[kv]
        s = jnp.where(no_prev, MASKED, s)
        es, ms = [], []
        for half in range(2):
            sh = s[:, half * 2 * w:(half + 1) * 2 * w]
            r = jnp.max(sh, axis=1, keepdims=True)
            m = jnp.maximum(jnp.broadcast_to(r, (rows, LANES)), sink_ref[kv, half])
            es.append(jnp.exp(sh - jnp.concatenate([m, m], axis=1)))
            ms.append(m)
        p = jnp.concatenate(es, axis=1).astype(BF16)
        o = jnp.dot(p, vv_aug, preferred_element_type=F32)
        m_mix = jnp.where(out_low, ms[0], ms[1])
        sink_mix = jnp.where(out_low, sink_ref[kv, 0], sink_ref[kv, 1])
        o = o[:, 0:LANES] / (o[:, LANES:2 * LANES] + jnp.exp(sink_mix - m_mix))
        for pp in range(npair):
            o_ref[:, base + pp * LANES:base + (pp + 1) * LANES] = (
                o[pp * w:(pp + 1) * w, :].astype(o_ref.dtype))


def _swa_attention(q, kvdup, sinks, t):
    w = SWA_WINDOW
    dq = SWA_Q_HEADS * SWA_HEAD_DIM
    dkv = SWA_KV_HEADS * LANES
    npair = SWA_GROUP // 2
    sink_tab = jnp.broadcast_to(
        sinks.astype(F32).reshape(SWA_KV_HEADS, npair, 2).transpose(0, 2, 1)[:, :, :, None, None],
        (SWA_KV_HEADS, 2, npair, w, LANES)).reshape(SWA_KV_HEADS, 2, npair * w, LANES)
    prev = lambda n: jnp.maximum(n - 1, 0)
    return pl.pallas_call(
        _swa_kernel,
        grid=(t // w,),
        in_specs=[
            pl.BlockSpec((w, dq), lambda n: (n, 0)),
            pl.BlockSpec((w, dkv), lambda n: (prev(n), 0)),
            pl.BlockSpec((w, dkv), lambda n: (n, 0)),
            pl.BlockSpec((w, dkv), lambda n: (prev(n), 1)),
            pl.BlockSpec((w, dkv), lambda n: (n, 1)),
            pl.BlockSpec((SWA_KV_HEADS, npair * w, 4 * w), lambda n: (0, 0, 0)),
            pl.BlockSpec((SWA_KV_HEADS, 2, npair * w, LANES), lambda n: (0, 0, 0, 0)),
        ],
        out_specs=pl.BlockSpec((w, dq), lambda n: (n, 0)),
        out_shape=jax.ShapeDtypeStruct((t, dq), BF16),
        compiler_params=_params(("arbitrary",), 48),
        name="swa_attention",
    )(q, kvdup, kvdup, kvdup, kvdup, _swa_bias_const(), sink_tab)


def kernel(x, c, norm_g, ada_w, ada_b, ffn_w_gate, ffn_w_up, ffn_w_down, moba_w_qkv, moba_w_o,
           pool_w, pool_scale, swa_w_qkv, swa_w_o, swa_sinks, final_g):
    b, t, d = x.shape
    assert b == 1
    depth = norm_g.shape[0]
    mods = _ada_mod(c, ada_w, ada_b)

    wg, wu, wd = ffn_w_gate, ffn_w_up, ffn_w_down
    moba_o_bf16 = moba_w_o.astype(BF16)
    swa_o_bf16 = swa_w_o.astype(BF16)

    def ffn(xs, i, s, mod3, final_norm):
        return _ffn(xs, norm_g[i, 2 * s], mod3, wg, wu, wd, i, s, final_g, final_norm=final_norm)

    xs = x.reshape(t, d)
    for i in range(depth):
        mod = mods[i]
        last = i == depth - 1
        xs = ffn(xs, i, 0, mod[0:3], False)
        kind, j = i % 3, i // 3
        m2 = mod[3:6]
        gate2 = mod[5:6]
        if kind == 0:
            qscale = jnp.concatenate([jnp.full((d,), MOBA_HEAD_DIM ** -0.5 * LOG2E, F32), jnp.ones((2 * d,), F32)])
            qkv = _norm_mod_matmul(xs, norm_g[i, 1], m2, moba_w_qkv, j, qscale, head_major=True)
            o = _moba_attention(qkv, t)
            xs = _proj_residual(o, moba_o_bf16, j, xs, gate2)
        elif kind == 1:
            xs = _pool_mixer(xs, norm_g[i, 1], m2, pool_w[j].astype(BF16), pool_scale[j])
        else:
            dq = SWA_Q_HEADS * SWA_HEAD_DIM
            hd = SWA_HEAD_DIM
            wqkv = swa_w_qkv[j]
            wq = wqkv[:, :dq]
            wk = wqkv[:, dq:dq + SWA_KV_HEADS * hd].reshape(d, SWA_KV_HEADS, 1, hd)
            wv = wqkv[:, dq + SWA_KV_HEADS * hd:].reshape(d, SWA_KV_HEADS, 1, hd)
            dup = lambda a: jnp.broadcast_to(a, (d, SWA_KV_HEADS, 2, hd)).reshape(d, SWA_KV_HEADS * 2 * hd)
            wkv = jnp.concatenate([dup(wk), dup(wv)], axis=1)
            q = _norm_mod_matmul(xs, norm_g[i, 1], m2, wq[None], 0,
                                 jnp.full((dq,), hd ** -0.5, F32))
            kvdup = _norm_mod_matmul(xs, norm_g[i, 1], m2, wkv[None], 0, jnp.ones((wkv.shape[1],), F32))
            o = _swa_attention(q, kvdup, swa_sinks[j], t)
            xs = _proj_residual(o, swa_o_bf16, j, xs, gate2)
        xs = ffn(xs, i, 1, mod[6:9], last)
    return xs.reshape(b, t, d)
```

```python
import functools

import numpy as np
import jax
import jax.numpy as jnp
from jax import lax
from jax.experimental import pallas as pl
from jax.experimental.pallas import tpu as pltpu

F32 = jnp.float32
BF16 = jnp.bfloat16

NORM_EPS = 1e-6
FFN_RES = 0.5
MASKED = -1e30
MAX_FLOOR = -1e20
LOG2E = 1.4426950408889634

LANES = 128
V7X_VMEM_BYTES = 64 * 1024 * 1024

MOBA_HEADS = 16
MOBA_HEAD_DIM = 128
MOBA_BLOCK = 256
MOBA_TOPK = 3
MOBA_QTILE = 2 * MOBA_BLOCK
MOBA_KTILE = 4 * MOBA_BLOCK
MOBA_MAX_BLOCKS = 32

POOL_WINDOWS = (2, 4, 8, 16)
POOL_HALO = 16

SWA_HEAD_DIM = 64
SWA_Q_HEADS = 32
SWA_KV_HEADS = 4
SWA_GROUP = SWA_Q_HEADS // SWA_KV_HEADS
SWA_WINDOW = 128

NT_DIMS = (((1,), (1,)), ((), ()))


def _params(semantics, vmem_mib):
    return pltpu.CompilerParams(dimension_semantics=semantics,
                                vmem_limit_bytes=vmem_mib * 1024 * 1024)


def _norm_mod(xf, g, shift, scale):
    ms = jnp.mean(xf * xf, axis=-1, keepdims=True)
    y = xf * lax.rsqrt(ms + NORM_EPS)
    return y * (g * (1.0 + scale)) + shift


def _split_bf16(v):
    hi = v.astype(BF16).astype(F32)
    lo = (v - hi).astype(BF16).astype(F32)
    return hi, lo


def _ada_kernel(c_ref, w_ref, b_ref, o_ref):
    cb = c_ref[...]
    cs = cb * jax.nn.sigmoid(cb)
    w = w_ref[...]
    prod = w * jnp.concatenate([cs] * (w.shape[1] // LANES), axis=1)
    o_ref[...] = jnp.sum(prod, axis=0, keepdims=True) + b_ref[...]


def _ada_mod(c, ada_w, ada_b, tn=1024):
    depth, d, n = ada_w.shape
    c_rep = jnp.broadcast_to(c.reshape(d, 1), (d, LANES))
    out = pl.pallas_call(
        _ada_kernel,
        grid=(depth, n // tn),
        in_specs=[
            pl.BlockSpec((d, LANES), lambda l, j: (0, 0)),
            pl.BlockSpec((None, d, tn), lambda l, j: (l, 0, j)),
            pl.BlockSpec((None, 1, tn), lambda l, j: (l, 0, j)),
        ],
        out_specs=pl.BlockSpec((None, 1, tn), lambda l, j: (l, 0, j)),
        out_shape=jax.ShapeDtypeStruct((depth, 1, n), F32),
        compiler_params=_params(("arbitrary", "arbitrary"), 40),
        name="ada_mod",
    )(c_rep, ada_w, ada_b.reshape(depth, 1, n))
    return out.reshape(depth, n // d, d)


def _ffn_kernel(x_ref, g_ref, mod_ref, wg_ref, wu_ref, wd_ref, fg_ref, o_ref, h_ref, *, final_norm):
    j = pl.program_id(1)
    last = pl.num_programs(1) - 1

    def partial_out(h):
        gg = jnp.dot(h, wg_ref[...].astype(BF16), preferred_element_type=F32)
        uu = jnp.dot(h, wu_ref[...].astype(BF16), preferred_element_type=F32)
        a = ((gg * jax.nn.sigmoid(gg)) * uu).astype(BF16)
        return jnp.dot(a, wd_ref[...].astype(BF16), preferred_element_type=F32)

    @pl.when(j == 0)
    def _():
        h = _norm_mod(x_ref[...], g_ref[...], mod_ref[0:1, :], mod_ref[1:2, :]).astype(BF16)
        h_ref[...] = h
        o_ref[...] = partial_out(h)

    @pl.when(jnp.logical_and(j > 0, j < last))
    def _():
        o_ref[...] += partial_out(h_ref[...])

    @pl.when(j == last)
    def _():
        acc = o_ref[...] + partial_out(h_ref[...])
        out = x_ref[...] + (FFN_RES * mod_ref[2:3, :]) * acc
        if final_norm:
            ms = jnp.mean(out * out, axis=-1, keepdims=True)
            out = (out * lax.rsqrt(ms + NORM_EPS)) * fg_ref[...]
        o_ref[...] = out


def _ffn(x, g, mod3, wg, wu, wd, layer, slot, final_g, *, final_norm, tm=1024, tf=256):
    t, d = x.shape
    f = wg.shape[3]
    return pl.pallas_call(
        functools.partial(_ffn_kernel, final_norm=final_norm),
        grid=(t // tm, f // tf),
        in_specs=[
            pl.BlockSpec((tm, d), lambda i, j: (i, 0)),
            pl.BlockSpec((1, d), lambda i, j: (0, 0)),
            pl.BlockSpec((3, d), lambda i, j: (0, 0)),
            pl.BlockSpec((None, None, d, tf), lambda i, j: (layer, slot, 0, j)),
            pl.BlockSpec((None, None, d, tf), lambda i, j: (layer, slot, 0, j)),
            pl.BlockSpec((None, None, tf, d), lambda i, j: (layer, slot, j, 0)),
            pl.BlockSpec((1, d), lambda i, j: (0, 0)),
        ],
        out_specs=pl.BlockSpec((tm, d), lambda i, j: (i, 0)),
        out_shape=jax.ShapeDtypeStruct((t, d), F32),
        scratch_shapes=[pltpu.VMEM((tm, d), BF16)],
        compiler_params=_params(("arbitrary", "arbitrary"), 60),
        name="ffn",
    )(x, g.reshape(1, d), mod3, wg, wu, wd, final_g.reshape(1, d))


def _nmm_kernel(x_ref, g_ref, mod_ref, w_ref, cs_ref, o_ref, h_ref, *, head_major):
    def project(h):
        r = jnp.dot(h, w_ref[...].astype(BF16), preferred_element_type=F32)
        r = (r * cs_ref[...]).astype(o_ref.dtype)
        if head_major:
            for hh in range(o_ref.shape[0]):
                o_ref[hh] = r[:, hh * LANES:(hh + 1) * LANES]
        else:
            o_ref[...] = r

    @pl.when(pl.program_id(1) == 0)
    def _():
        h = _norm_mod(x_ref[...], g_ref[...], mod_ref[0:1, :], mod_ref[1:2, :]).astype(BF16)
        h_ref[...] = h
        project(h)

    @pl.when(pl.program_id(1) > 0)
    def _():
        project(h_ref[...])


def _norm_mod_matmul(x, g, mod3, w, widx, colscale, *, head_major=False, tm=1024, tn=1024):
    t, d = x.shape
    n = w.shape[2]
    if head_major:
        out_spec = pl.BlockSpec((tn // LANES, tm, LANES), lambda i, j: (j, i, 0))
        out_shape = jax.ShapeDtypeStruct((n // LANES, t, LANES), BF16)
    else:
        out_spec = pl.BlockSpec((tm, tn), lambda i, j: (i, j))
        out_shape = jax.ShapeDtypeStruct((t, n), BF16)
    return pl.pallas_call(
        functools.partial(_nmm_kernel, head_major=head_major),
        grid=(t // tm, n // tn),
        in_specs=[
            pl.BlockSpec((tm, d), lambda i, j: (i, 0)),
            pl.BlockSpec((1, d), lambda i, j: (0, 0)),
            pl.BlockSpec((3, d), lambda i, j: (0, 0)),
            pl.BlockSpec((None, d, tn), lambda i, j: (widx, 0, j)),
            pl.BlockSpec((1, tn), lambda i, j: (0, j)),
        ],
        out_specs=out_spec,
        out_shape=out_shape,
        scratch_shapes=[pltpu.VMEM((tm, d), BF16)],
        compiler_params=_params(("arbitrary", "arbitrary"), 48),
        name="norm_mod_matmul",
    )(x, g.reshape(1, d), mod3, w, colscale.reshape(1, n))


def _proj_res_kernel(a_ref, w_ref, x_ref, gate_ref, o_ref):
    r = jnp.dot(a_ref[...], w_ref[...], preferred_element_type=F32)
    o_ref[...] = x_ref[...] + gate_ref[...] * r


def _proj_residual(a, w, widx, x, gate, *, tm=512):
    t, k = a.shape
    d = w.shape[2]
    return pl.pallas_call(
        _proj_res_kernel,
        grid=(t // tm,),
        in_specs=[
            pl.BlockSpec((tm, k), lambda i: (i, 0)),
            pl.BlockSpec((None, k, d), lambda i: (widx, 0, 0)),
            pl.BlockSpec((tm, d), lambda i: (i, 0)),
            pl.BlockSpec((1, d), lambda i: (0, 0)),
        ],
        out_specs=pl.BlockSpec((tm, d), lambda i: (i, 0)),
        out_shape=jax.ShapeDtypeStruct((t, d), F32),
        compiler_params=_params(("arbitrary",), 48),
        name="proj_residual",
    )(a, w, x, gate)


def _moba_key_const(t):
    pos = np.arange(t)
    lane = np.arange(LANES)[None, :]
    blk = (pos // MOBA_BLOCK)[:, None]
    inblk = (pos % MOBA_BLOCK)[:, None].astype(np.float32)
    kc = np.where(lane < 64, (blk == (lane % 32)).astype(np.float32), 0.0)
    kc = np.where((lane == 64) | (lane == 65), 1.0, kc)
    kc = np.where((lane == 66) | (lane == 67), inblk, kc)
    return jnp.asarray(kc.T, dtype=BF16)


def _alibi_slopes(n):
    return np.asarray([2.0 ** (-8.0 * (i + 1) / n) for i in range(n)], np.float32)


def _moba_kernel(slope_ref, q_ref, k_ref, v_ref, kc_ref, o_ref,
                 kaug_t, vaug, qaug, kmean_s, s_ring, mb_ring, acc_s, *, nblk):
    dh = MOBA_HEAD_DIM
    tq, tk = MOBA_QTILE, MOBA_KTILE
    qt = pl.program_id(1)
    n_past = (qt * tq) // tk
    n_diag = max(tq // tk, 1)
    causal_shift = qt * tq - n_past * tk

    @pl.when(qt == 0)
    def _():
        k = k_ref[...]
        for j in range(kaug_t.shape[0]):
            kaug_t[j, 0:dh, :] = k[j * tk:(j + 1) * tk, :].astype(F32).T.astype(BF16)
            kaug_t[j, dh:2 * dh, :] = kc_ref[:, j * tk:(j + 1) * tk]
        vaug[:, 0:dh] = v_ref[...]
        vaug[:, dh:2 * dh] = jnp.ones((vaug.shape[0], dh), BF16)
        kf = k.astype(F32).reshape(nblk, MOBA_BLOCK, dh)
        kmean_s[...] = jnp.zeros(kmean_s.shape, F32)
        kmean_s[0:nblk, :] = jnp.sum(kf, axis=1) * (1.0 / MOBA_BLOCK)

    q = q_ref[...]
    km_hi, km_lo = _split_bf16(kmean_s[...])
    gate_t = (lax.dot_general(km_hi.astype(BF16), q, NT_DIMS, preferred_element_type=F32)
              + lax.dot_general(km_lo.astype(BF16), q, NT_DIMS, preferred_element_type=F32))
    blk_t = lax.broadcasted_iota(jnp.int32, (MOBA_MAX_BLOCKS, tq), 0)
    row_t = lax.broadcasted_iota(jnp.int32, (MOBA_MAX_BLOCKS, tq), 1)
    qblk_t = (tq // MOBA_BLOCK) * qt + row_t // MOBA_BLOCK
    blk_tf = blk_t.astype(F32)
    neg_inf = -jnp.inf

    g = jnp.where(blk_t < qblk_t, gate_t[0:MOBA_MAX_BLOCKS, :], neg_inf)
    act_t = jnp.where(blk_t == qblk_t, 1.0, 0.0)
    for _ in range(MOBA_TOPK):
        mx = jnp.max(g, axis=0, keepdims=True)
        idx = jnp.min(jnp.where(g == mx, blk_tf, float(LANES)), axis=0, keepdims=True)
        hit = blk_tf == jnp.where(mx > neg_inf, idx, float(LANES))
        act_t = jnp.where(hit, 1.0, act_t)
        g = jnp.where(hit, neg_inf, g)
    pad = jnp.zeros((LANES - 2 * MOBA_MAX_BLOCKS, tq), F32)
    active = jnp.concatenate([act_t, act_t, pad], axis=0).T > 0.5

    lane = lax.broadcasted_iota(jnp.int32, (tq, LANES), 1)
    row = lax.broadcasted_iota(jnp.int32, (tq, LANES), 0)
    blk = lane & (MOBA_MAX_BLOCKS - 1)
    slope = slope_ref[...]
    offset = -slope * (tq * qt - MOBA_BLOCK * blk).astype(F32)
    off_hi, off_lo = _split_bf16(offset)
    row_hi, row_lo = _split_bf16(-slope * row.astype(F32))
    sl_hi, sl_lo = _split_bf16(jnp.broadcast_to(slope, (tq, LANES)))
    extra = jnp.where(lane < 32, jnp.where(active, off_hi, MASKED),
            jnp.where(lane < 64, jnp.where(active, off_lo, 0.0),
            jnp.where(lane == 64, row_hi,
            jnp.where(lane == 65, row_lo,
            jnp.where(lane == 66, sl_hi,
            jnp.where(lane == 67, sl_lo, 0.0))))))
    qaug[:, 0:dh] = q
    qaug[:, dh:2 * dh] = extra.astype(BF16)

    def key_tile(u):
        if n_diag == 1:
            return jnp.where(u == 0, n_past, jnp.maximum(jnp.minimum(u, n_past) - 1, 0))
        past = jnp.maximum(jnp.minimum(u, n_past + n_diag - 1) - n_diag, 0)
        return jnp.where(u < n_diag, n_past + u, past)

    def scores(u):
        return jnp.dot(qaug[...], kaug_t[key_tile(u)], preferred_element_type=F32)

    def diag_scores(u):
        r2 = lax.broadcasted_iota(jnp.int32, (tq, tk), 0)
        c2 = lax.broadcasted_iota(jnp.int32, (tq, tk), 1)
        return jnp.where(c2 + u * tk <= r2 + causal_shift, scores(u), MASKED)

    def running_max(s, prev):
        r = jnp.max(s, axis=1, keepdims=True)
        return jnp.maximum(prev, jnp.broadcast_to(r, (tq, LANES)))

    floor = jnp.full((tq, LANES), MAX_FLOOR, F32)
    assert n_diag <= 2
    s0 = diag_scores(0)
    s_ring[0] = s0
    mb_ring[0] = running_max(s0, floor)
    mb_ring[1] = floor
    s_ring[1] = diag_scores(1) if n_diag == 2 else scores(1)
    acc_s[...] = jnp.zeros(acc_s.shape, F32)

    def step(c, carry):
        cur, nxt, nxt2 = lax.rem(c, 3), lax.rem(c + 1, 3), lax.rem(c + 2, 3)
        mcur, mprev = lax.rem(c, 2), lax.rem(c + 1, 2)
        mb = mb_ring[mcur]
        alpha = jnp.exp2(mb_ring[mprev] - mb)
        tv = pl.multiple_of(key_tile(c) * tk, tk)
        vt = vaug[pl.ds(tv, tk), :]
        for lo in range(0, tq, MOBA_BLOCK):
            rows = slice(lo, lo + MOBA_BLOCK)
            mbh = mb[rows]
            p = jnp.exp2(s_ring[cur, rows, :] - jnp.concatenate([mbh] * (tk // LANES), axis=1)).astype(BF16)
            pv = jnp.dot(p, vt, preferred_element_type=F32)
            ah = alpha[rows]
            acc_s[rows, :] = acc_s[rows, :] * jnp.concatenate([ah, ah], axis=1) + pv
        mb_ring[mprev] = running_max(s_ring[nxt], mb)
        kt = kaug_t[key_tile(c + 2)]
        for lo in range(0, tq, MOBA_BLOCK):
            rows = slice(lo, lo + MOBA_BLOCK)
            s_ring[nxt2, rows, :] = jnp.dot(qaug[rows, :], kt, preferred_element_type=F32)
        return carry

    lax.fori_loop(0, n_past + n_diag, step, 0)

    acc = acc_s[...]
    o_ref[...] = (acc[:, 0:dh] / acc[:, dh:2 * dh]).astype(o_ref.dtype)


def _moba_attention(qkv, t):
    h, dh, tq, tk = MOBA_HEADS, MOBA_HEAD_DIM, MOBA_QTILE, MOBA_KTILE
    nblk = t // MOBA_BLOCK
    assert t % tk == 0 and t % tq == 0 and (tk % tq == 0 or tq % tk == 0) and nblk <= MOBA_MAX_BLOCKS
    slopes = _alibi_slopes(h).astype(np.float64) * LOG2E
    slopes = jnp.asarray(np.broadcast_to(slopes.astype(np.float32)[:, None, None], (h, 1, LANES)))
    return pl.pallas_call(
        functools.partial(_moba_kernel, nblk=nblk),
        grid=(h, t // tq),
        in_specs=[
            pl.BlockSpec((None, 1, LANES), lambda hh, p: (hh, 0, 0)),
            pl.BlockSpec((None, tq, dh), lambda hh, p: (hh, p, 0)),
            pl.BlockSpec((None, t, dh), lambda hh, p: (h + hh, 0, 0)),
            pl.BlockSpec((None, t, dh), lambda hh, p: (2 * h + hh, 0, 0)),
            pl.BlockSpec((LANES, t), lambda hh, p: (0, 0)),
        ],
        out_specs=pl.BlockSpec((tq, dh), lambda hh, p: (p, hh)),
        out_shape=jax.ShapeDtypeStruct((t, h * dh), BF16),
        scratch_shapes=[
            pltpu.VMEM((t // tk, 2 * dh, tk), BF16),
            pltpu.VMEM((t, 2 * dh), BF16),
            pltpu.VMEM((tq, 2 * dh), BF16),
            pltpu.VMEM((LANES, dh), F32),
            pltpu.VMEM((3, tq, tk), F32),
            pltpu.VMEM((2, tq, LANES), F32),
            pltpu.VMEM((tq, 2 * dh), F32),
        ],
        compiler_params=_params(("arbitrary", "arbitrary"), 48),
        name="moba_attention",
    )(slopes, qkv, qkv, qkv, _moba_key_const(t))


def _pool_kernel(x_ref, xh_ref, g_ref, mod_ref, w_ref, ps_ref, o_ref, hbuf):
    i = pl.program_id(0)
    tm = x_ref.shape[0]
    grp = w_ref.shape[1]
    g, shift, scale, gate = g_ref[...], mod_ref[0:1, :], mod_ref[1:2, :], mod_ref[2:3, :]
    x = x_ref[...]
    h = _norm_mod(x, g, shift, scale)
    halo = _norm_mod(xh_ref[...], g, shift, scale)
    hbuf[0:POOL_HALO, :] = jnp.where(i == 0, 0.0, halo)
    hbuf[POOL_HALO:POOL_HALO + tm, :] = h
    tpos = i * tm + lax.broadcasted_iota(jnp.int32, (tm, 1), 0)
    for gi, win in enumerate(POOL_WINDOWS):
        cols = slice(gi * grp, (gi + 1) * grp)
        hg = h[:, cols]
        acc = hg
        for back in range(1, win):
            acc = acc + hbuf[POOL_HALO - back:POOL_HALO - back + tm, cols]
        cnt = jnp.minimum(tpos + 1, win).astype(F32)
        pooled = acc / cnt - hg
        mixed = jnp.dot(pooled.astype(BF16), w_ref[gi], preferred_element_type=F32)
        o_ref[:, cols] = x[:, cols] + gate[:, cols] * (mixed * ps_ref[:, cols])


def _pool_mixer(x, g, mod3, w_pool, pool_scale, *, tm=512):
    t, d = x.shape
    ng, grp, _ = w_pool.shape
    halo_blocks = tm // POOL_HALO
    return pl.pallas_call(
        _pool_kernel,
        grid=(t // tm,),
        in_specs=[
            pl.BlockSpec((tm, d), lambda i: (i, 0)),
            pl.BlockSpec((POOL_HALO, d), lambda i: (jnp.maximum(i * halo_blocks - 1, 0), 0)),
            pl.BlockSpec((1, d), lambda i: (0, 0)),
            pl.BlockSpec((3, d), lambda i: (0, 0)),
            pl.BlockSpec((ng, grp, grp), lambda i: (0, 0, 0)),
            pl.BlockSpec((1, d), lambda i: (0, 0)),
        ],
        out_specs=pl.BlockSpec((tm, d), lambda i: (i, 0)),
        out_shape=jax.ShapeDtypeStruct((t, d), F32),
        scratch_shapes=[pltpu.VMEM((POOL_HALO + tm, d), F32)],
        compiler_params=_params(("arbitrary",), 48),
        name="pool_mixer",
    )(x, x, g.reshape(1, d), mod3, w_pool, pool_scale.reshape(1, d))


def _swa_bias_const():
    w = SWA_WINDOW
    r = np.arange(w)[:, None]
    c = np.arange(2 * w)[None, :]
    dist = (r + w - c).astype(np.float32)
    valid = (dist >= 0) & (dist < w)
    slopes = _alibi_slopes(SWA_Q_HEADS)
    npair = SWA_GROUP // 2
    out = np.empty((SWA_KV_HEADS, npair * w, 2 * 2 * w), np.float32)
    for kv in range(SWA_KV_HEADS):
        for p in range(npair):
            for half in range(2):
                sl = slopes[kv * SWA_GROUP + 2 * p + half]
                tile = np.where(valid, -sl * dist, np.float32(MASKED)).astype(np.float32)
                out[kv, p * w:(p + 1) * w, half * 2 * w:(half + 1) * 2 * w] = tile
    return jnp.asarray(out)


def _swa_kernel(q_ref, kp_ref, kc_ref, vp_ref, vc_ref, bias_ref, sink_ref, o_ref):
    n = pl.program_id(0)
    w = SWA_WINDOW
    npair = SWA_GROUP // 2
    rows = npair * w
    lane = lax.broadcasted_iota(jnp.int32, (2 * w, LANES), 1)
    low = lane < SWA_HEAD_DIM
    col = lax.broadcasted_iota(jnp.int32, (rows, 4 * w), 1)
    prev_cols = (col & (2 * w - 1)) < w
    no_prev = jnp.logical_and(n == 0, prev_cols)
    out_low = lax.broadcasted_iota(jnp.int32, (rows, LANES), 1) < SWA_HEAD_DIM
    ones_even = jnp.where(low, 1.0, 0.0)
    ones_blk = jnp.concatenate([ones_even, 1.0 - ones_even], axis=0)
    for kv in range(SWA_KV_HEADS):
        ksl = slice(kv * LANES, (kv + 1) * LANES)
        kd = jnp.concatenate([kp_ref[:, ksl], kc_ref[:, ksl]], axis=0).astype(F32)
        vd = jnp.concatenate([vp_ref[:, ksl], vc_ref[:, ksl]], axis=0).astype(F32)
        kk = jnp.concatenate([jnp.where(low, kd, 0.0), jnp.where(low, 0.0, kd)], axis=0).astype(BF16)
        vv = jnp.concatenate([jnp.where(low, vd, 0.0), jnp.where(low, 0.0, vd)], axis=0)
        vv_aug = jnp.concatenate([vv, ones_blk], axis=1).astype(BF16)
        base = kv * npair * LANES
        qs = jnp.concatenate(
            [q_ref[:, base + p * LANES:base + (p + 1) * LANES] for p in range(npair)], axis=0)
        s = lax.dot_general(qs, kk, NT_DIMS, preferred_element_type=F32) + bias_ref[kv]
        s = jnp.where(no_prev, MASKED, s)
        es, ms = [], []
        for half in range(2):
            sh = s[:, half * 2 * w:(half + 1) * 2 * w]
            r = jnp.max(sh, axis=1, keepdims=True)
            m = jnp.maximum(jnp.broadcast_to(r, (rows, LANES)), sink_ref[kv, half])
            es.append(jnp.exp(sh - jnp.concatenate([m, m], axis=1)))
            ms.append(m)
        p = jnp.concatenate(es, axis=1).astype(BF16)
        o = jnp.dot(p, vv_aug, preferred_element_type=F32)
        m_mix = jnp.where(out_low, ms[0], ms[1])
        sink_mix = jnp.where(out_low, sink_ref[kv, 0], sink_ref[kv, 1])
        o = o[:, 0:LANES] / (o[:, LANES:2 * LANES] + jnp.exp(sink_mix - m_mix))
        for pp in range(npair):
            o_ref[:, base + pp * LANES:base + (pp + 1) * LANES] = (
                o[pp * w:(pp + 1) * w, :].astype(o_ref.dtype))


def _swa_attention(q, kvdup, sinks, t):
    w = SWA_WINDOW
    dq = SWA_Q_HEADS * SWA_HEAD_DIM
    dkv = SWA_KV_HEADS * LANES
    npair = SWA_GROUP // 2
    sink_tab = jnp.broadcast_to(
        sinks.astype(F32).reshape(SWA_KV_HEADS, npair, 2).transpose(0, 2, 1)[:, :, :, None, None],
        (SWA_KV_HEADS, 2, npair, w, LANES)).reshape(SWA_KV_HEADS, 2, npair * w, LANES)
    prev = lambda n: jnp.maximum(n - 1, 0)
    return pl.pallas_call(
        _swa_kernel,
        grid=(t // w,),
        in_specs=[
            pl.BlockSpec((w, dq), lambda n: (n, 0)),
            pl.BlockSpec((w, dkv), lambda n: (prev(n), 0)),
            pl.BlockSpec((w, dkv), lambda n: (n, 0)),
            pl.BlockSpec((w, dkv), lambda n: (prev(n), 1)),
            pl.BlockSpec((w, dkv), lambda n: (n, 1)),
            pl.BlockSpec((SWA_KV_HEADS, npair * w, 4 * w), lambda n: (0, 0, 0)),
            pl.BlockSpec((SWA_KV_HEADS, 2, npair * w, LANES), lambda n: (0, 0, 0, 0)),
        ],
        out_specs=pl.BlockSpec((w, dq), lambda n: (n, 0)),
        out_shape=jax.ShapeDtypeStruct((t, dq), BF16),
        compiler_params=_params(("arbitrary",), 48),
        name="swa_attention",
    )(q, kvdup, kvdup, kvdup, kvdup, _swa_bias_const(), sink_tab)


def kernel(x, c, norm_g, ada_w, ada_b, ffn_w_gate, ffn_w_up, ffn_w_down, moba_w_qkv, moba_w_o,
           pool_w, pool_scale, swa_w_qkv, swa_w_o, swa_sinks, final_g):
    b, t, d = x.shape
    assert b == 1
    depth = norm_g.shape[0]
    mods = _ada_mod(c, ada_w, ada_b)

    wg, wu, wd = ffn_w_gate, ffn_w_up, ffn_w_down
    moba_o_bf16 = moba_w_o.astype(BF16)
    swa_o_bf16 = swa_w_o.astype(BF16)

    def ffn(xs, i, s, mod3, final_norm):
        return _ffn(xs, norm_g[i, 2 * s], mod3, wg, wu, wd, i, s, final_g, final_norm=final_norm)

    xs = x.reshape(t, d)
    for i in range(depth):
        mod = mods[i]
        last = i == depth - 1
        xs = ffn(xs, i, 0, mod[0:3], False)
        kind, j = i % 3, i // 3
        m2 = mod[3:6]
        gate2 = mod[5:6]
        if kind == 0:
            qscale = jnp.concatenate([jnp.full((d,), MOBA_HEAD_DIM ** -0.5 * LOG2E, F32), jnp.ones((2 * d,), F32)])
            qkv = _norm_mod_matmul(xs, norm_g[i, 1], m2, moba_w_qkv, j, qscale, head_major=True)
            o = _moba_attention(qkv, t)
            xs = _proj_residual(o, moba_o_bf16, j, xs, gate2)
        elif kind == 1:
            xs = _pool_mixer(xs, norm_g[i, 1], m2, pool_w[j].astype(BF16), pool_scale[j])
        else:
            dq = SWA_Q_HEADS * SWA_HEAD_DIM
            hd = SWA_HEAD_DIM
            wqkv = swa_w_qkv[j]
            wq = wqkv[:, :dq]
            wk = wqkv[:, dq:dq + SWA_KV_HEADS * hd].reshape(d, SWA_KV_HEADS, 1, hd)
            wv = wqkv[:, dq + SWA_KV_HEADS * hd:].reshape(d, SWA_KV_HEADS, 1, hd)
            dup = lambda a: jnp.broadcast_to(a, (d, SWA_KV_HEADS, 2, hd)).reshape(d, SWA_KV_HEADS * 2 * hd)
            wkv = jnp.concatenate([dup(wk), dup(wv)], axis=1)
            q = _norm_mod_matmul(xs, norm_g[i, 1], m2, wq[None], 0,
                                 jnp.full((dq,), hd ** -0.5, F32))
            kvdup = _norm_mod_matmul(xs, norm_g[i, 1], m2, wkv[None], 0, jnp.ones((wkv.shape[1],), F32))
            o = _swa_attention(q, kvdup, swa_sinks[j], t)
            xs = _proj_residual(o, swa_o_bf16, j, xs, gate2)
        xs = ffn(xs, i, 1, mod[6:9], last)
    return xs.reshape(b, t, d)
```
